```python
import math
import jax, jax.numpy as jnp
from jax import lax
import numpy as np

D_MODEL = 1024
BATCH = 8
SEQ = 2048
DEPTH = 4
DEC_BATCH = 128
DEC_SEQ = 4
PAST_LEN = 2048
PAGE_SIZE = 128

HEAD_DIM = 64
H_A = 4
H_B = 4
H_C = 4
WIDTH_A = H_A * 2 * HEAD_DIM
WIDTH_B = H_B * HEAD_DIM
WIDTH_C = H_C * HEAD_DIM
N_BRANCH = 3
MOBA_BLOCK = 256
MOBA_TOPK = 3
MOBA_QCHUNK = 32
ATTN_QBLOCK = 128
D_FF = 4 * D_MODEL
ROPE_THETA = 10000.0
NORM_EPS = 1e-6
IN_SPLITS = (2 * H_A * HEAD_DIM, 2 * H_A * HEAD_DIM, WIDTH_A, WIDTH_B, WIDTH_B, WIDTH_B, WIDTH_C, WIDTH_C, WIDTH_C, H_C, N_BRANCH * D_MODEL)
D_IN = sum(IN_SPLITS)

kernel_name = 'hybrid_diff_moba_fox_decoder_step'


def _rmsnorm(x, g):
    xf = x.astype(jnp.float32)
    y = xf * lax.rsqrt(jnp.mean(xf * xf, axis=-1, keepdims=True) + NORM_EPS)
    return (y * g.astype(jnp.float32)).astype(x.dtype)


def _rope(x, pos):
    d = x.shape[-1]
    half = d // 2
    inv = jnp.power(ROPE_THETA, -jnp.arange(half, dtype=jnp.float32) * 2.0 / d)
    ang = pos.astype(jnp.float32)[:, None] * inv[None, :]
    shape = (1, pos.shape[0]) + (1,) * (x.ndim - 3) + (half,)
    cos = jnp.cos(ang).reshape(shape)
    sin = jnp.sin(ang).reshape(shape)
    xf = x.astype(jnp.float32)
    x1, x2 = xf[..., :half], xf[..., half:]
    return jnp.concatenate([x1 * cos - x2 * sin, x2 * cos + x1 * sin], axis=-1).astype(x.dtype)


def _sweep(body, qb, q_arrays, q_pos):
    t = q_pos.shape[0]
    n = t // qb
    xs = tuple(jnp.swapaxes(a.reshape((a.shape[0], n, qb) + a.shape[2:]), 0, 1) for a in q_arrays)
    out = lax.map(lambda args: body(*args), xs + (q_pos.reshape(n, qb),))
    out = jnp.swapaxes(out, 0, 1)
    return out.reshape((out.shape[0], t) + out.shape[3:])


def _diff_attention(q12, k12, v, q_pos, k_pos, lam):
    scale = q12.shape[-1] ** -0.5
    def body(qb, pb):
        s = jnp.einsum('bqchd,bkchd->bchqk', qb, k12).astype(jnp.float32) * scale
        mask = k_pos[None, :] <= pb[:, None]
        p = jax.nn.softmax(jnp.where(mask, s, -jnp.inf), axis=-1)
        w = p[:, 0] - lam * p[:, 1]
        return jnp.einsum('bhqk,bkhe->bqhe', w.astype(v.dtype), v)
    return _sweep(body, min(ATTN_QBLOCK, q_pos.shape[0]), (q12,), q_pos)


def _forgetting_attention(q, k, v, fq, fk, q_pos, k_pos):
    scale = q.shape[-1] ** -0.5
    fk_t = jnp.swapaxes(fk, 1, 2)[:, :, None, :]
    def body(qb, fqb, pb):
        s = jnp.einsum('bqhd,bkhd->bhqk', qb, k).astype(jnp.float32) * scale
        s = s + jnp.swapaxes(fqb, 1, 2)[..., None] - fk_t
        mask = k_pos[None, :] <= pb[:, None]
        p = jax.nn.softmax(jnp.where(mask, s, -jnp.inf), axis=-1)
        return jnp.einsum('bhqk,bkhd->bqhd', p.astype(v.dtype), v)
    return _sweep(body, min(ATTN_QBLOCK, q_pos.shape[0]), (q, fq), q_pos)


def _moba_attention(q, k, v, q_pos):
    b, l, h, d = k.shape
    nb = -(-l // MOBA_BLOCK)
    pad = ((0, 0), (0, nb * MOBA_BLOCK - l), (0, 0), (0, 0))
    kb = jnp.pad(k, pad).reshape(b, nb, MOBA_BLOCK, h, d).transpose(0, 3, 1, 2, 4)
    vb = jnp.pad(v, pad).reshape(b, nb, MOBA_BLOCK, h, d).transpose(0, 3, 1, 2, 4)
    kmean = jnp.mean(kb.astype(jnp.float32), axis=3)
    n_top = min(MOBA_TOPK, nb)
    bi = jnp.arange(b)[:, None, None, None]
    hi = jnp.arange(h)[None, :, None, None]
    blk_ids = jnp.arange(nb)
    offs = jnp.arange(MOBA_BLOCK)
    scale = d ** -0.5
    def body(qc, pc):
        qt = jnp.swapaxes(qc, 1, 2)
        nq = pc.shape[0]
        own = pc // MOBA_BLOCK
        gate = jnp.einsum('bhqd,bhnd->bhqn', qt.astype(jnp.float32), kmean)
        gate = jnp.where(blk_ids[None, :] < own[:, None], gate, -jnp.inf)
        top = lax.top_k(gate, n_top)[1]
        own_idx = jnp.broadcast_to(own[None, None, :, None], top.shape[:3] + (1,)).astype(top.dtype)
        sel = jnp.concatenate([top, own_idx], axis=-1)
        kg = kb[bi, hi, sel]
        vg = vb[bi, hi, sel]
        s = jnp.einsum('bhqd,bhqnkd->bhqnk', qt, kg).astype(jnp.float32) * scale
        top_ok = jnp.broadcast_to((jnp.arange(n_top)[None, :] < own[:, None])[:, :, None], (nq, n_top, MOBA_BLOCK))
        own_ok = (own[:, None] * MOBA_BLOCK + offs[None, :] <= pc[:, None])[:, None, :]
        valid = jnp.concatenate([top_ok, own_ok], axis=1)
        s = jnp.where(valid[None, None], s, -jnp.inf)
        p = jax.nn.softmax(s.reshape(s.shape[:3] + (-1,)), axis=-1).reshape(s.shape)
        return jnp.einsum('bhqnk,bhqnkd->bqhd', p.astype(vg.dtype), vg)
    return _sweep(body, min(MOBA_QCHUNK, q_pos.shape[0]), (q,), q_pos)


def _layer(x, c, past, lam_init, w_mod, b_mod, norm_g, w_in, b_forget, diff_lambda, diff_subln_g,
           w_branch_a, w_branch_b, w_branch_c, w_out, w_ff1, w_ff2):
    b, t, _ = x.shape
    p_len = 0 if past is None else past[0].shape[1]
    q_pos = p_len + jnp.arange(t)
    k_pos = jnp.arange(p_len + t)
    mod = jax.nn.silu(c) @ w_mod + b_mod
    sh1, sc1, g1, sh2, sc2, g2 = [m[:, None, :] for m in jnp.split(mod, 6, axis=-1)]
    h = _rmsnorm(x, norm_g[0]) * (1 + sc1) + sh1
    offsets = [int(o) for o in np.cumsum(IN_SPLITS)[:-1]]
    qa, ka, va, qb, kb, vb, qc, kc, vc, fl, gl = jnp.split(h @ w_in, offsets, axis=-1)
    qa = _rope(qa.reshape(b, t, 2, H_A, HEAD_DIM), q_pos)
    ka = _rope(ka.reshape(b, t, 2, H_A, HEAD_DIM), q_pos)
    va = va.reshape(b, t, H_A, 2 * HEAD_DIM)
    qb = _rope(qb.reshape(b, t, H_B, HEAD_DIM), q_pos)
    kb = _rope(kb.reshape(b, t, H_B, HEAD_DIM), q_pos)
    vb = vb.reshape(b, t, H_B, HEAD_DIM)
    qc = qc.reshape(b, t, H_C, HEAD_DIM)
    kc = kc.reshape(b, t, H_C, HEAD_DIM)
    vc = vc.reshape(b, t, H_C, HEAD_DIM)
    lf = jax.nn.log_sigmoid(fl.astype(jnp.float32) + b_forget.astype(jnp.float32)).astype(x.dtype)
    new = (ka, va, kb, vb, kc, vc, lf)
    if past is None:
        full = new
    else:
        full = tuple(jnp.concatenate([pst, nw.astype(pst.dtype)], axis=1) for pst, nw in zip(past, new))
    ka_all, va_all, kb_all, vb_all, kc_all, vc_all, lf_all = full
    lv = diff_lambda.astype(jnp.float32)
    lam = jnp.exp(jnp.sum(lv[0] * lv[1])) - jnp.exp(jnp.sum(lv[2] * lv[3])) + lam_init
    ya = _diff_attention(qa, ka_all, va_all, q_pos, k_pos, lam)
    ya = _rmsnorm(ya, diff_subln_g) * (1 - lam_init)
    yb = _moba_attention(qb, kb_all, vb_all, q_pos)
    f_all = jnp.cumsum(lf_all.astype(jnp.float32), axis=1)
    yc = _forgetting_attention(qc, kc_all, vc_all, f_all[:, p_len:], f_all, q_pos, k_pos)
    ga, gb, gc = jnp.split(jax.nn.sigmoid(gl), N_BRANCH, axis=-1)
    merged = (ga * (ya.reshape(b, t, WIDTH_A) @ w_branch_a)
              + gb * (yb.reshape(b, t, WIDTH_B) @ w_branch_b)
              + gc * (yc.reshape(b, t, WIDTH_C) @ w_branch_c))
    x = x + g1 * _rmsnorm(merged @ w_out, norm_g[1])
    h = _rmsnorm(x, norm_g[2]) * (1 + sc2) + sh2
    ff = jnp.square(jax.nn.relu(h @ w_ff1)) @ w_ff2
    x = x + g2 * _rmsnorm(ff, norm_g[3])
    return x, new


def _gather_pages(cache, page_table, layer):
    g = cache[page_table, layer]
    return g.reshape((g.shape[0], g.shape[1] * g.shape[2]) + g.shape[3:])


def setup_inputs(seed: int = 0) -> dict:
    key = jax.random.key(seed)
    ks = jax.random.split(key, 32)
    n_pages = PAST_LEN // PAGE_SIZE
    n_pool = (DEC_BATCH * n_pages * 5) // 4
    def nrm(k, shape, scale=1.0):
        return jax.random.normal(k, shape, jnp.float32) * scale
    return {
        'x_prompt': nrm(ks[0], (BATCH, SEQ, D_MODEL)),
        'x_sample': nrm(ks[1], (DEC_BATCH, DEC_SEQ, D_MODEL)),
        'cache_a_k': nrm(ks[2], (n_pool, DEPTH, PAGE_SIZE, 2, H_A, HEAD_DIM)),
        'cache_a_v': nrm(ks[3], (n_pool, DEPTH, PAGE_SIZE, H_A, 2 * HEAD_DIM)),
        'cache_b_k': nrm(ks[4], (n_pool, DEPTH, PAGE_SIZE, H_B, HEAD_DIM)),
        'cache_b_v': nrm(ks[5], (n_pool, DEPTH, PAGE_SIZE, H_B, HEAD_DIM)),
        'cache_c_k': nrm(ks[6], (n_pool, DEPTH, PAGE_SIZE, H_C, HEAD_DIM)),
        'cache_c_v': nrm(ks[7], (n_pool, DEPTH, PAGE_SIZE, H_C, HEAD_DIM)),
        'cache_c_logf': jax.nn.log_sigmoid(nrm(ks[8], (n_pool, DEPTH, PAGE_SIZE, H_C), 0.5) + 3.0),
        'page_table': jax.random.permutation(ks[9], n_pool)[:DEC_BATCH * n_pages].reshape(DEC_BATCH, n_pages).astype(jnp.int32),
        'c_prompt': nrm(ks[10], (BATCH, D_MODEL)),
        'c_sample': nrm(ks[11], (DEC_BATCH, D_MODEL)),
        'w_mod': nrm(ks[12], (DEPTH, D_MODEL, 6 * D_MODEL), 0.5 * D_MODEL ** -0.5),
        'b_mod': nrm(ks[13], (DEPTH, 6 * D_MODEL), 0.01),
        'norm_g': 1.0 + nrm(ks[14], (DEPTH, 4, D_MODEL), 0.05),
        'w_in': nrm(ks[15], (DEPTH, D_MODEL, D_IN), D_MODEL ** -0.5),
        'b_forget': 3.0 + nrm(ks[16], (DEPTH, H_C), 0.5),
        'diff_lambda': nrm(ks[17], (DEPTH, 4, HEAD_DIM), 0.1),
        'diff_subln_g': 1.0 + nrm(ks[18], (DEPTH, 2 * HEAD_DIM), 0.05),
        'w_branch_a': nrm(ks[19], (DEPTH, WIDTH_A, D_MODEL), WIDTH_A ** -0.5),
        'w_branch_b': nrm(ks[20], (DEPTH, WIDTH_B, D_MODEL), WIDTH_B ** -0.5),
        'w_branch_c': nrm(ks[21], (DEPTH, WIDTH_C, D_MODEL), WIDTH_C ** -0.5),
        'w_out': nrm(ks[22], (DEPTH, D_MODEL, D_MODEL), D_MODEL ** -0.5),
        'w_ff1': nrm(ks[23], (DEPTH, D_MODEL, D_FF), D_MODEL ** -0.5),
        'w_ff2': nrm(ks[24], (DEPTH, D_FF, D_MODEL), D_FF ** -0.5),
    }


def reference(x_prompt, x_sample, cache_a_k, cache_a_v, cache_b_k, cache_b_v, cache_c_k, cache_c_v, cache_c_logf,
              page_table, c_prompt, c_sample, w_mod, b_mod, norm_g, w_in, b_forget, diff_lambda, diff_subln_g,
              w_branch_a, w_branch_b, w_branch_c, w_out, w_ff1, w_ff2):
    caches = (cache_a_k, cache_a_v, cache_b_k, cache_b_v, cache_c_k, cache_c_v, cache_c_logf)
    y_prompt, y_sample = x_prompt, x_sample
    rows_p, rows_s = [], []
    for l in range(DEPTH):
        lam_init = 0.8 - 0.6 * math.exp(-0.3 * l)
        wl = (w_mod[l], b_mod[l], norm_g[l], w_in[l], b_forget[l], diff_lambda[l], diff_subln_g[l],
              w_branch_a[l], w_branch_b[l], w_branch_c[l], w_out[l], w_ff1[l], w_ff2[l])
        y_prompt, new_p = _layer(y_prompt, c_prompt, None, lam_init, *wl)
        past = tuple(_gather_pages(cache, page_table, l) for cache in caches)
        y_sample, new_s = _layer(y_sample, c_sample, past, lam_init, *wl)
        rows_p.append(new_p)
        rows_s.append(new_s)
    sp = [jnp.stack([r[i] for r in rows_p], axis=1) for i in range(len(caches))]
    ss = [jnp.stack([r[i] for r in rows_s], axis=1) for i in range(len(caches))]
    return (y_prompt, y_sample, sp[0], sp[1], sp[2], sp[3], sp[4], sp[5], sp[6],
            ss[0], ss[1], ss[2], ss[3], ss[4], ss[5], ss[6])
```

```python
import functools
import math

import jax
import jax.numpy as jnp
from jax import lax
from jax.experimental import pallas as pl
from jax.experimental.pallas import tpu as pltpu

HEAD_DIM = 64
N_HEADS = 4
WIDTH_A = N_HEADS * 2 * HEAD_DIM
WIDTH_B = N_HEADS * HEAD_DIM
WIDTH_C = N_HEADS * HEAD_DIM
QKV_WIDTH = 3 * WIDTH_A + 3 * WIDTH_B + 3 * WIDTH_C
MOBA_BLOCK = 256
MOBA_TOPK = 3
ROPE_THETA = 10000.0
NORM_EPS = 1e-6
LANES = 128
SUBLANES = 8
DEC_GROUPS = 8
ROW_TILE = 256
ATTN_BLOCK = 256
VMEM_LIMIT = 56 * 1024 * 1024

F32 = jnp.float32
BF16 = jnp.bfloat16
HIGHEST = lax.Precision.HIGHEST
NT_DIMS = (((1,), (1,)), ((), ()))


def _params(*sem):
    return pltpu.CompilerParams(dimension_semantics=sem, vmem_limit_bytes=VMEM_LIMIT)


def _rms(x, g):
    return x * lax.rsqrt(jnp.mean(x * x, axis=-1, keepdims=True) + NORM_EPS) * g


def _sigmoid(x):
    return 1.0 / (1.0 + jnp.exp(-x))


def _nt_dot(a, b, precision=None):
    return lax.dot_general(a, b, NT_DIMS, precision=precision, preferred_element_type=F32)


def _mod_kernel(c_ref, w_ref, b_ref, o_ref):
    c = c_ref[...]
    s = c * _sigmoid(c)
    o_ref[...] = jnp.dot(s, w_ref[...], precision=HIGHEST, preferred_element_type=F32) + b_ref[...]


def _modulation(c_all, w_mod, b_mod):
    depth, d, n6 = w_mod.shape
    m = c_all.shape[0]
    tn = d
    assert n6 % tn == 0
    return pl.pallas_call(
        _mod_kernel,
        grid=(depth, n6 // tn),
        in_specs=[
            pl.BlockSpec((m, d), lambda l, j: (0, 0)),
            pl.BlockSpec((None, d, tn), lambda l, j: (l, 0, j)),
            pl.BlockSpec((None, 1, tn), lambda l, j: (l, 0, j)),
        ],
        out_specs=pl.BlockSpec((None, m, tn), lambda l, j: (l, 0, j)),
        out_shape=jax.ShapeDtypeStruct((depth, m, n6), F32),
        compiler_params=_params("arbitrary", "arbitrary"),
    )(c_all, w_mod, b_mod.reshape(depth, 1, n6))


def _rope_chunks(z, cos, sin):
    lane = lax.broadcasted_iota(jnp.int32, (z.shape[0], LANES), 1)
    first_half = (lane % HEAD_DIM) < (HEAD_DIM // 2)
    out = []
    for c in range(z.shape[1] // LANES):
        zc = z[:, c * LANES:(c + 1) * LANES]
        rot = jnp.where(first_half, pltpu.roll(zc, LANES - HEAD_DIM // 2, 1), pltpu.roll(zc, HEAD_DIM // 2, 1))
        out.append(zc * cos + rot * sin)
    return jnp.concatenate(out, axis=1)


def _log_sigmoid(z):
    return jnp.minimum(z, 0.0) - jnp.log1p(jnp.exp(-jnp.abs(z)))


def _inproj_kernel(x_ref, sc_ref, sh_ref, g_ref, cos_ref, sin_ref, wqkv_ref, wf_ref, bf_ref, tri_ref,
                   q_ref, qbf_ref, kvb_ref, kvf_ref, lf_ref, cum_ref, kmean_ref, carry_ref, *, tiles_per_seq):
    i = pl.program_id(0)

    @pl.when(i % tiles_per_seq == 0)
    def _():
        carry_ref[...] = jnp.zeros_like(carry_ref)

    x = x_ref[...]
    h = _rms(x, g_ref[...]) * (1.0 + sc_ref[...]) + sh_ref[...]
    hb = h.astype(BF16)
    y = jnp.dot(hb, wqkv_ref[...], preferred_element_type=F32)
    cos = cos_ref[...]
    sin = sin_ref[...]
    scale = HEAD_DIM ** -0.5
    o = 0
    qa = _rope_chunks(y[:, o:o + WIDTH_A], cos, sin) * scale; o += WIDTH_A
    ka = _rope_chunks(y[:, o:o + WIDTH_A], cos, sin); o += WIDTH_A
    va = y[:, o:o + WIDTH_A]; o += WIDTH_A
    qb = _rope_chunks(y[:, o:o + WIDTH_B], cos, sin) * scale; o += WIDTH_B
    kb = _rope_chunks(y[:, o:o + WIDTH_B], cos, sin); o += WIDTH_B
    vb = y[:, o:o + WIDTH_B]; o += WIDTH_B
    qc = y[:, o:o + WIDTH_C] * scale; o += WIDTH_C
    kc = y[:, o:o + WIDTH_C]; o += WIDTH_C
    vc = y[:, o:o + WIDTH_C]

    q_ref[...] = jnp.concatenate([qa, qb, qc], axis=1).astype(BF16)
    qbf_ref[...] = qb
    kv = jnp.concatenate([ka, va, kb, vb, kc, vc], axis=1)
    kvf_ref[...] = kv
    kvb_ref[...] = kv.astype(BF16)
    kmean_ref[...] = jnp.mean(kb, axis=0, keepdims=True)

    lf = _log_sigmoid(jnp.dot(hb, wf_ref[...], preferred_element_type=F32) + bf_ref[...])
    lf_ref[...] = lf
    cum = jnp.dot(tri_ref[...], lf, precision=HIGHEST, preferred_element_type=F32) + carry_ref[...]
    cum_ref[...] = cum
    carry_ref[...] = cum[cum.shape[0] - 1:, :]


def _inproj(x2, sc, sh, g, cos, sin, wqkv, wf, bfg, *, rows_per_seq, per_row_mod):
    n, d = x2.shape
    tm = min(ROW_TILE, n)
    n_tiles = n // tm
    tiles_per_seq = max(rows_per_seq // tm, 1)
    tri = (lax.broadcasted_iota(jnp.int32, (tm, tm), 0) >= lax.broadcasted_iota(jnp.int32, (tm, tm), 1)).astype(F32)
    if per_row_mod:
        mod_spec = pl.BlockSpec((tm, d), lambda i: (i, 0))
        tab_spec = pl.BlockSpec((tm, LANES), lambda i: (i, 0))
    else:
        mod_spec = pl.BlockSpec((None, 1, d), lambda i: (i // tiles_per_seq, 0, 0))
        tab_spec = pl.BlockSpec((tm, LANES), lambda i: (i % tiles_per_seq, 0))
    const = lambda i: (0, 0)
    row = lambda w: pl.BlockSpec((tm, w), lambda i: (i, 0))
    kv_w = 2 * WIDTH_A + 2 * WIDTH_B + 2 * WIDTH_C
    q_w = WIDTH_A + WIDTH_B + WIDTH_C
    return pl.pallas_call(
        functools.partial(_inproj_kernel, tiles_per_seq=tiles_per_seq),
        grid=(n_tiles,),
        in_specs=[
            row(d), mod_spec, mod_spec,
            pl.BlockSpec((1, d), const),
            tab_spec, tab_spec,
            pl.BlockSpec((d, QKV_WIDTH), const),
            pl.BlockSpec((d, LANES), const),
            pl.BlockSpec((1, LANES), const),
            pl.BlockSpec((tm, tm), const),
        ],
        out_specs=[
            row(q_w), row(WIDTH_B), row(kv_w), row(kv_w), row(LANES), row(LANES),
            pl.BlockSpec((None, 1, WIDTH_B), lambda i: (i, 0, 0)),
        ],
        out_shape=[
            jax.ShapeDtypeStruct((n, q_w), BF16),
            jax.ShapeDtypeStruct((n, WIDTH_B), F32),
            jax.ShapeDtypeStruct((n, kv_w), BF16),
            jax.ShapeDtypeStruct((n, kv_w), F32),
            jax.ShapeDtypeStruct((n, LANES), F32),
            jax.ShapeDtypeStruct((n, LANES), F32),
            jax.ShapeDtypeStruct((n_tiles, 1, WIDTH_B), F32),
        ],
        scratch_shapes=[pltpu.VMEM((1, LANES), F32)],
        compiler_params=_params("arbitrary"),
    )(x2, sc, sh, g, cos, sin, wqkv, wf, bfg, tri)


def _head_lane_mask(width, head):
    lane = lax.broadcasted_iota(jnp.int32, (1, width), 1)
    return (lane // HEAD_DIM) == head


def _flash_head(q, k_ref, v_ref, kcol, vcol, vwidth, qi, bias_fn, block):
    row = lax.broadcasted_iota(jnp.int32, (block, block), 0)
    col = lax.broadcasted_iota(jnp.int32, (block, block), 1)

    def scores(j):
        start = pl.multiple_of(j * block, block)
        ks = k_ref[pl.ds(start, block), kcol:kcol + LANES]
        vs = v_ref[pl.ds(start, block), vcol:vcol + vwidth]
        return bias_fn(j, _nt_dot(q, ks)), vs

    s, vs = scores(qi)
    s = jnp.where(row >= col, s, -jnp.inf)
    m = jnp.max(s, axis=-1, keepdims=True)
    p = jnp.exp(s - m)
    l = jnp.sum(p, axis=-1, keepdims=True)
    acc = jnp.dot(p.astype(BF16), vs, preferred_element_type=F32)

    def body(j, carry):
        m, l, acc = carry
        s, vs = scores(j)
        m_new = jnp.maximum(m, jnp.max(s, axis=-1, keepdims=True))
        alpha = jnp.exp(m - m_new)
        p = jnp.exp(s - m_new)
        l = alpha * l + jnp.sum(p, axis=-1, keepdims=True)
        acc = alpha * acc + jnp.dot(p.astype(BF16), vs, preferred_element_type=F32)
        return m_new, l, acc

    m, l, acc = lax.fori_loop(0, qi, body, (m, l, acc))
    return acc, l


def _attn_a_kernel(q_ref, k_ref, v_ref, lam_ref, g_ref, o_ref, *, lam_init, block):
    qi = pl.program_id(1)
    lv = lam_ref[...]
    lam = (jnp.exp(jnp.sum(lv[0:1] * lv[1:2], axis=-1, keepdims=True))
           - jnp.exp(jnp.sum(lv[2:3] * lv[3:4], axis=-1, keepdims=True)) + lam_init)
    no_bias = lambda j, s: s
    vw = 2 * HEAD_DIM
    for h in range(N_HEADS):
        outs = []
        for c in range(2):
            g = c * N_HEADS + h
            kcol = (g // 2) * LANES
            qg = q_ref[:, kcol:kcol + LANES]
            qg = jnp.where(_head_lane_mask(LANES, g % 2), qg, jnp.zeros_like(qg))
            acc, l = _flash_head(qg, k_ref, v_ref, kcol, h * vw, vw, qi, no_bias, block)
            outs.append(acc / l)
        y = outs[0] - lam * outs[1]
        y = _rms(y, g_ref[...]) * (1.0 - lam_init)
        o_ref[:, h * vw:(h + 1) * vw] = y.astype(o_ref.dtype)


def _attn_a(q3, kvb3, diff_lambda, subln_g, lam_init):
    b, t, _ = q3.shape
    blk = min(ATTN_BLOCK, t)
    return pl.pallas_call(
        functools.partial(_attn_a_kernel, lam_init=lam_init, block=blk),
        grid=(b, t // blk),
        in_specs=[
            pl.BlockSpec((None, blk, WIDTH_A), lambda bi, qi: (bi, qi, 0)),
            pl.BlockSpec((None, t, WIDTH_A), lambda bi, qi: (bi, 0, 0)),
            pl.BlockSpec((None, t, WIDTH_A), lambda bi, qi: (bi, 0, 1)),
            pl.BlockSpec((4, HEAD_DIM), lambda bi, qi: (0, 0)),
            pl.BlockSpec((1, 2 * HEAD_DIM), lambda bi, qi: (0, 0)),
        ],
        out_specs=pl.BlockSpec((None, blk, WIDTH_A), lambda bi, qi: (bi, qi, 0)),
        out_shape=jax.ShapeDtypeStruct((b, t, WIDTH_A), BF16),
        compiler_params=_params("arbitrary", "arbitrary"),
    )(q3, kvb3, kvb3, diff_lambda, subln_g)


def _topk_block_mask(gate, valid):
    nb = gate.shape[1]
    blk = lax.broadcasted_iota(jnp.int32, gate.shape, 1)
    g = jnp.where(valid, gate, -jnp.inf)
    rank = jnp.zeros(gate.shape, jnp.int32)
    for m in range(nb):
        gm = g[:, m:m + 1]
        ahead = (gm > g) | ((gm == g) & (m < blk))
        rank = rank + ahead.astype(jnp.int32)
    return (valid & (rank < MOBA_TOPK)).astype(F32)


def _attn_b_kernel(q_ref, qf_ref, k_ref, v_ref, kmean_ref, o_ref, *, block):
    qi = pl.program_id(1)
    nb = kmean_ref.shape[0]
    kmean = kmean_ref[...]
    qf = qf_ref[...]
    blk_id = lax.broadcasted_iota(jnp.int32, (block, nb), 1)
    valid = blk_id < qi
    pair_out = []
    for h in range(N_HEADS):
        qh = jnp.where(_head_lane_mask(WIDTH_B, h), qf, 0.0)
        gate = _nt_dot(qh, kmean, precision=HIGHEST)
        sel = _topk_block_mask(gate, valid)

        def bias(j, s, sel=sel):
            chosen = jnp.sum(jnp.where(blk_id == j, sel, 0.0), axis=-1, keepdims=True)
            return jnp.where((chosen > 0.5) | (j == qi), s, -jnp.inf)

        kcol = (h // 2) * LANES
        qg = q_ref[:, kcol:kcol + LANES]
        qg = jnp.where(_head_lane_mask(LANES, h % 2), qg, jnp.zeros_like(qg))
        acc, l = _flash_head(qg, k_ref, v_ref, kcol, kcol, LANES, qi, bias, block)
        pair_out.append(acc / l)
        if h % 2 == 1:
            y = jnp.where(_head_lane_mask(LANES, 0), pair_out[0], pair_out[1])
            o_ref[:, kcol:kcol + LANES] = y.astype(o_ref.dtype)
            pair_out = []


def _attn_b(q3, qbf3, kvb3, kmean3):
    b, t, _ = q3.shape
    blk = MOBA_BLOCK
    nb = t // blk
    q_off = WIDTH_A // WIDTH_B
    k_off = 2 * WIDTH_A // WIDTH_B
    return pl.pallas_call(
        functools.partial(_attn_b_kernel, block=blk),
        grid=(b, nb),
        in_specs=[
            pl.BlockSpec((None, blk, WIDTH_B), lambda bi, qi: (bi, qi, q_off)),
            pl.BlockSpec((None, blk, WIDTH_B), lambda bi, qi: (bi, qi, 0)),
            pl.BlockSpec((None, t, WIDTH_B), lambda bi, qi: (bi, 0, k_off)),
            pl.BlockSpec((None, t, WIDTH_B), lambda bi, qi: (bi, 0, k_off + 1)),
            pl.BlockSpec((None, nb, WIDTH_B), lambda bi, qi: (bi, 0, 0)),
        ],
        out_specs=pl.BlockSpec((None, blk, WIDTH_B), lambda bi, qi: (bi, qi, 0)),
        out_shape=jax.ShapeDtypeStruct((b, t, WIDTH_B), BF16),
        compiler_params=_params("arbitrary", "arbitrary"),
    )(q3, qbf3, kvb3, kvb3, kmean3)


def _attn_c_kernel(q_ref, k_ref, v_ref, fq_ref, fk_ref, o_ref, *, block):
    qi = pl.program_id(1)
    fq = fq_ref[...]
    pair_out = []
    for h in range(N_HEADS):
        fq_h = fq[:, h:h + 1]

        def bias(j, s, h=h, fq_h=fq_h):
            return s + (fq_h - fk_ref[j, h:h + 1, :])

        kcol = (h // 2) * LANES
        qg = q_ref[:, kcol:kcol + LANES]
        qg = jnp.where(_head_lane_mask(LANES, h % 2), qg, jnp.zeros_like(qg))
        acc, l = _flash_head(qg, k_ref, v_ref, kcol, kcol, LANES, qi, bias, block)
        pair_out.append(acc / l)
        if h % 2 == 1:
            y = jnp.where(_head_lane_mask(LANES, 0), pair_out[0], pair_out[1])
            o_ref[:, kcol:kcol + LANES] = y.astype(o_ref.dtype)
            pair_out = []


def _attn_c(q3, kvb3, cum3, cum_t):
    b, t, _ = q3.shape
    blk = min(ATTN_BLOCK, t)
    nk = t // blk
    q_off = (WIDTH_A + WIDTH_B) // WIDTH_C
    k_off = (2 * WIDTH_A + 2 * WIDTH_B) // WIDTH_C
    return pl.pallas_call(
        functools.partial(_attn_c_kernel, block=blk),
        grid=(b, nk),
        in_specs=[
            pl.BlockSpec((None, blk, WIDTH_C), lambda bi, qi: (bi, qi, q_off)),
            pl.BlockSpec((None, t, WIDTH_C), lambda bi, qi: (bi, 0, k_off)),
            pl.BlockSpec((None, t, WIDTH_C), lambda bi, qi: (bi, 0, k_off + 1)),
            pl.BlockSpec((None, blk, LANES), lambda bi, qi: (bi, qi, 0)),
            pl.BlockSpec((None, nk, SUBLANES, blk), lambda bi, qi: (bi, 0, 0, 0)),
        ],
        out_specs=pl.BlockSpec((None, blk, WIDTH_C), lambda bi, qi: (bi, qi, 0)),
        out_shape=jax.ShapeDtypeStruct((b, t, WIDTH_C), BF16),
        compiler_params=_params("arbitrary", "arbitrary"),
    )(q3, kvb3, kvb3, cum3, cum_t)


def _tail_page(new_ref, page):
    new = new_ref[...]
    return jnp.concatenate([new, jnp.zeros((page - new.shape[0], new.shape[1]), new.dtype)], axis=0)


def _tail_mask(rows, page, n_new):
    tok = lax.broadcasted_iota(jnp.int32, (rows, page), 0) // DEC_GROUPS
    key = lax.broadcasted_iota(jnp.int32, (rows, page), 1)
    return (key < n_new) & (key <= tok)


def _dec_softmax(s_pages):
    m = functools.reduce(jnp.maximum, [jnp.max(s, axis=-1, keepdims=True) for s in s_pages])
    e_pages = [jnp.exp(s - m) for s in s_pages]
    l = functools.reduce(lambda a, b: a + b, [jnp.sum(e, axis=-1, keepdims=True) for e in e_pages])
    return [e.astype(BF16) for e in e_pages], l


def _sum_list(xs):
    return functools.reduce(lambda a, b: a + b, xs)


def _collapse_groups(z, n_new):
    return jnp.sum(z.reshape(n_new, DEC_GROUPS, z.shape[1]), axis=1)


def _dec_a_kernel(pt_ref, q_ref, kn_ref, vn_ref, lam_ref, g_ref, *rest, n_pages, n_new, lam_init):
    kt_refs, v_refs, o_ref = rest[:n_pages], rest[n_pages:2 * n_pages], rest[2 * n_pages]
    page = kt_refs[0].shape[1]
    vw = 2 * HEAD_DIM
    q = q_ref[...]
    rows = q.shape[0]
    s_pages = [jnp.dot(q, r[...].astype(BF16), preferred_element_type=F32) for r in kt_refs]
    s_tail = _nt_dot(q, _tail_page(kn_ref, page).astype(BF16))
    s_pages.append(jnp.where(_tail_mask(rows, page, n_new), s_tail, -jnp.inf))
    e_pages, l = _dec_softmax(s_pages)
    v_tail = _tail_page(vn_ref, page).astype(BF16)
    acc_heads = []
    for h in range(N_HEADS):
        parts = [jnp.dot(e_pages[p], v_refs[p][pl.ds(h, page, stride=N_HEADS), :].astype(BF16),
                         preferred_element_type=F32) for p in range(n_pages)]
        parts.append(jnp.dot(e_pages[n_pages], v_tail[:, h * vw:(h + 1) * vw], preferred_element_type=F32))
        acc_heads.append(_sum_list(parts))
    acc = jnp.concatenate(acc_heads, axis=1)
    lv = lam_ref[...]
    lam = (jnp.exp(jnp.sum(lv[0:1] * lv[1:2], axis=-1, keepdims=True))
           - jnp.exp(jnp.sum(lv[2:3] * lv[3:4], axis=-1, keepdims=True)) + lam_init)
    grp = lax.broadcasted_iota(jnp.int32, acc.shape, 0) % DEC_GROUPS
    vhead = lax.broadcasted_iota(jnp.int32, acc.shape, 1) // vw
    coef = jnp.where(grp < N_HEADS, 1.0, -lam)
    z = jnp.where((grp % N_HEADS) == vhead, (acc / l) * coef, 0.0)
    y = _collapse_groups(z, n_new)
    g = g_ref[...]
    y = jnp.concatenate([_rms(y[:, h * vw:(h + 1) * vw], g) for h in range(N_HEADS)], axis=1) * (1.0 - lam_init)
    o_ref[...] = y.astype(o_ref.dtype)


def _page_specs(n_pages, layer, rows, cols):
    return [pl.BlockSpec((None, None, rows, cols), lambda bi, pt, p=p: (pt[bi, p], layer, 0, 0))
            for p in range(n_pages)]


def _dec_a(page_table, qbd, k_new, v_new, cache_kt, cache_v, diff_lambda, subln_g, layer, lam_init, n_new):
    nb, rows, _ = qbd.shape
    n_pages = page_table.shape[1]
    page = cache_kt.shape[3]
    pad = k_new.shape[1]
    grid_spec = pltpu.PrefetchScalarGridSpec(
        num_scalar_prefetch=1,
        grid=(nb,),
        in_specs=[
            pl.BlockSpec((None, rows, WIDTH_A), lambda bi, pt: (bi, 0, 0)),
            pl.BlockSpec((None, pad, WIDTH_A), lambda bi, pt: (bi, 0, 0)),
            pl.BlockSpec((None, pad, WIDTH_A), lambda bi, pt: (bi, 0, 0)),
            pl.BlockSpec((4, HEAD_DIM), lambda bi, pt: (0, 0)),
            pl.BlockSpec((1, 2 * HEAD_DIM), lambda bi, pt: (0, 0)),
        ] + _page_specs(n_pages, layer, WIDTH_A, page) + _page_specs(n_pages, layer, page * N_HEADS, 2 * HEAD_DIM),
        out_specs=pl.BlockSpec((None, n_new, WIDTH_A), lambda bi, pt: (bi, 0, 0)),
    )
    return pl.pallas_call(
        functools.partial(_dec_a_kernel, n_pages=n_pages, n_new=n_new, lam_init=lam_init),
        grid_spec=grid_spec,
        out_shape=jax.ShapeDtypeStruct((nb, n_new, WIDTH_A), BF16),
        compiler_params=_params("arbitrary"),
    )(page_table, qbd, k_new, v_new, diff_lambda, subln_g, *([cache_kt] * n_pages), *([cache_v] * n_pages))


def _dec_pv_t(e_pages, vt_refs, vn_ref, page):
    n_pages = len(vt_refs)
    parts = [_nt_dot(e_pages[p], vt_refs[p][...].astype(BF16)) for p in range(n_pages)]
    parts.append(jnp.dot(e_pages[n_pages], _tail_page(vn_ref, page).astype(BF16), preferred_element_type=F32))
    return _sum_list(parts)


def _dec_b_kernel(pt_ref, q_ref, qf_ref, kn_ref, vn_ref, *rest, n_pages, n_new):
    kt_refs, vt_refs, o_ref = rest[:n_pages], rest[n_pages:2 * n_pages], rest[2 * n_pages]
    page = kt_refs[0].shape[1]
    pages_per_block = MOBA_BLOCK // page
    n_blocks = n_pages // pages_per_block
    q = q_ref[...]
    rows = q.shape[0]
    kt_f32 = [r[...] for r in kt_refs]
    lane_id = lax.broadcasted_iota(jnp.int32, (WIDTH_B, n_blocks), 1)
    kmean_t = jnp.zeros((WIDTH_B, n_blocks), F32)
    for j in range(n_blocks):
        tot = _sum_list(kt_f32[j * pages_per_block:(j + 1) * pages_per_block])
        col = jnp.sum(tot, axis=1, keepdims=True) * (1.0 / MOBA_BLOCK)
        kmean_t = jnp.where(lane_id == j, col, kmean_t)
    gate = jnp.dot(qf_ref[...], kmean_t, precision=HIGHEST, preferred_element_type=F32)
    sel = _topk_block_mask(gate, jnp.full(gate.shape, True))
    s_pages = []
    for p in range(n_pages):
        s = jnp.dot(q, kt_f32[p].astype(BF16), preferred_element_type=F32)
        j = p // pages_per_block
        s_pages.append(jnp.where(sel[:, j:j + 1] > 0.5, s, -jnp.inf))
    s_tail = _nt_dot(q, _tail_page(kn_ref, page).astype(BF16))
    s_pages.append(jnp.where(_tail_mask(rows, page, n_new), s_tail, -jnp.inf))
    e_pages, l = _dec_softmax(s_pages)
    acc = _dec_pv_t(e_pages, vt_refs, vn_ref, page)
    grp = lax.broadcasted_iota(jnp.int32, acc.shape, 0) % DEC_GROUPS
    vhead = lax.broadcasted_iota(jnp.int32, acc.shape, 1) // HEAD_DIM
    z = jnp.where(grp == vhead, acc / l, 0.0)
    o_ref[...] = _collapse_groups(z, n_new).astype(o_ref.dtype)


def _dec_b(page_table, qbd, qbd_f32, k_new, v_new, cache_kt, cache_vt, layer, n_new):
    nb, rows, _ = qbd.shape
    n_pages = page_table.shape[1]
    page = cache_kt.shape[3]
    pad = k_new.shape[1]
    grid_spec = pltpu.PrefetchScalarGridSpec(
        num_scalar_prefetch=1,
        grid=(nb,),
        in_specs=[
            pl.BlockSpec((None, rows, WIDTH_B), lambda bi, pt: (bi, 0, 0)),
            pl.BlockSpec((None, rows, WIDTH_B), lambda bi, pt: (bi, 0, 0)),
            pl.BlockSpec((None, pad, WIDTH_B), lambda bi, pt: (bi, 0, 0)),
            pl.BlockSpec((None, pad, WIDTH_B), lambda bi, pt: (bi, 0, 0)),
        ] + _page_specs(n_pages, layer, WIDTH_B, page) + _page_specs(n_pages, layer, WIDTH_B, page),
        out_specs=pl.BlockSpec((None, n_new, WIDTH_B), lambda bi, pt: (bi, 0, 0)),
    )
    return pl.pallas_call(
        functools.partial(_dec_b_kernel, n_pages=n_pages, n_new=n_new),
        grid_spec=grid_spec,
        out_shape=jax.ShapeDtypeStruct((nb, n_new, WIDTH_B), BF16),
        compiler_params=_params("arbitrary"),
    )(page_table, qbd, qbd_f32, k_new, v_new, *([cache_kt] * n_pages), *([cache_vt] * n_pages))


def _dec_c_kernel(pt_ref, q_ref, kn_ref, vn_ref, lfn_ref, tri_ref, *rest, n_pages, n_new):
    kt_refs, vt_refs = rest[:n_pages], rest[n_pages:2 * n_pages]
    f_refs, o_ref = rest[2 * n_pages:3 * n_pages], rest[3 * n_pages]
    page = kt_refs[0].shape[1]
    q = q_ref[...]
    rows = q.shape[0]
    tri = tri_ref[...]
    carry = jnp.zeros((SUBLANES, 1), F32)
    cum_pages = []
    for lf in [r[...] for r in f_refs] + [lfn_ref[...]]:
        cum = jnp.dot(lf, tri, precision=HIGHEST, preferred_element_type=F32) + carry
        carry = cum[:, page - 1:page]
        cum_pages.append(cum)
    cum_new = jnp.concatenate([cum_pages[-1]] * n_new, axis=0)
    tok = lax.broadcasted_iota(jnp.int32, (rows, page), 0) // DEC_GROUPS
    key = lax.broadcasted_iota(jnp.int32, (rows, page), 1)
    f_query = jnp.sum(jnp.where(key == tok, cum_new, 0.0), axis=-1, keepdims=True)
    s_pages = []
    for p in range(n_pages + 1):
        if p < n_pages:
            s = jnp.dot(q, kt_refs[p][...].astype(BF16), preferred_element_type=F32)
        else:
            s = _nt_dot(q, _tail_page(kn_ref, page).astype(BF16))
        f_key = jnp.concatenate([cum_pages[p]] * n_new, axis=0)
        s_pages.append(s + (f_query - f_key))
    s_pages[-1] = jnp.where(_tail_mask(rows, page, n_new), s_pages[-1], -jnp.inf)
    e_pages, l = _dec_softmax(s_pages)
    acc = _dec_pv_t(e_pages, vt_refs, vn_ref, page)
    grp = lax.broadcasted_iota(jnp.int32, acc.shape, 0) % DEC_GROUPS
    vhead = lax.broadcasted_iota(jnp.int32, acc.shape, 1) // HEAD_DIM
    z = jnp.where(grp == vhead, acc / l, 0.0)
    o_ref[...] = _collapse_groups(z, n_new).astype(o_ref.dtype)


def _dec_c(page_table, qbd, k_new, v_new, lf_new_t, cache_kt, cache_vt, cache_lf_t, layer, n_new):
    nb, rows, _ = qbd.shape
    n_pages = page_table.shape[1]
    page = cache_kt.shape[3]
    pad = k_new.shape[1]
    tri = (lax.broadcasted_iota(jnp.int32, (page, page), 0) <= lax.broadcasted_iota(jnp.int32, (page, page), 1)).astype(F32)
    grid_spec = pltpu.PrefetchScalarGridSpec(
        num_scalar_prefetch=1,
        grid=(nb,),
        in_specs=[
            pl.BlockSpec((None, rows, WIDTH_C), lambda bi, pt: (bi, 0, 0)),
            pl.BlockSpec((None, pad, WIDTH_C), lambda bi, pt: (bi, 0, 0)),
            pl.BlockSpec((None, pad, WIDTH_C), lambda bi, pt: (bi, 0, 0)),
            pl.BlockSpec((None, SUBLANES, page), lambda bi, pt: (bi, 0, 0)),
            pl.BlockSpec((page, page), lambda bi, pt: (0, 0)),
        ] + _page_specs(n_pages, layer, WIDTH_C, page) + _page_specs(n_pages, layer, WIDTH_C, page)
        + _page_specs(n_pages, layer, SUBLANES, page),
        out_specs=pl.BlockSpec((None, n_new, WIDTH_C), lambda bi, pt: (bi, 0, 0)),
    )
    return pl.pallas_call(
        functools.partial(_dec_c_kernel, n_pages=n_pages, n_new=n_new),
        grid_spec=grid_spec,
        out_shape=jax.ShapeDtypeStruct((nb, n_new, WIDTH_C), BF16),
        compiler_params=_params("arbitrary"),
    )(page_table, qbd, k_new, v_new, lf_new_t, tri,
      *([cache_kt] * n_pages), *([cache_vt] * n_pages), *([cache_lf_t] * n_pages))


def _merge_kernel(x_ref, sc_ref, sh_ref, gate_ref, g0_ref, g1_ref, ya_ref, yb_ref, yc_ref,
                  wg_ref, wa_ref, wb_ref, wc_ref, wo_ref, o_ref):
    x = x_ref[...]
    d = x.shape[1]
    h = _rms(x, g0_ref[...]) * (1.0 + sc_ref[...]) + sh_ref[...]
    gl = jnp.dot(h.astype(BF16), wg_ref[...], preferred_element_type=F32)
    merged = (_sigmoid(gl[:, :d]) * jnp.dot(ya_ref[...], wa_ref[...], preferred_element_type=F32)
              + _sigmoid(gl[:, d:2 * d]) * jnp.dot(yb_ref[...], wb_ref[...], preferred_element_type=F32)
              + _sigmoid(gl[:, 2 * d:]) * jnp.dot(yc_ref[...], wc_ref[...], preferred_element_type=F32))
    o = jnp.dot(merged.astype(BF16), wo_ref[...], preferred_element_type=F32)
    o_ref[...] = x + gate_ref[...] * _rms(o, g1_ref[...])


def _mod_specs(tm, d, rows_per_seq, per_row_mod):
    if per_row_mod:
        return pl.BlockSpec((tm, d), lambda i: (i, 0))
    tiles_per_seq = rows_per_seq // tm
    return pl.BlockSpec((None, 1, d), lambda i: (i // tiles_per_seq, 0, 0))


def _merge(x2, sc, sh, gate, g0, g1, ya, yb, yc, wg, wa, wb, wc, wo, *, rows_per_seq, per_row_mod):
    n, d = x2.shape
    tm = min(ROW_TILE, n)
    mod_spec = _mod_specs(tm, d, rows_per_seq, per_row_mod)
    const = lambda i: (0, 0)
    row = lambda w: pl.BlockSpec((tm, w), lambda i: (i, 0))
    full = lambda a: pl.BlockSpec(a.shape, const)
    return pl.pallas_call(
        _merge_kernel,
        grid=(n // tm,),
        in_specs=[row(d), mod_spec, mod_spec, mod_spec, pl.BlockSpec((1, d), const), pl.BlockSpec((1, d), const),
                  row(WIDTH_A), row(WIDTH_B), row(WIDTH_C), full(wg), full(wa), full(wb), full(wc), full(wo)],
        out_specs=row(d),
        out_shape=jax.ShapeDtypeStruct((n, d), F32),
        compiler_params=_params("arbitrary"),
    )(x2, sc, sh, gate, g0, g1, ya, yb, yc, wg, wa, wb, wc, wo)


def _ffn_kernel(x_ref, sc_ref, sh_ref, gate_ref, g2_ref, g3_ref, w1_ref, w2_ref, o_ref, *, chunk):
    x = x_ref[...]
    h = (_rms(x, g2_ref[...]) * (1.0 + sc_ref[...]) + sh_ref[...]).astype(BF16)
    d_ff = w1_ref.shape[1]
    acc = None
    for j in range(d_ff // chunk):
        a = jnp.dot(h, w1_ref[:, j * chunk:(j + 1) * chunk], preferred_element_type=F32)
        a = jnp.square(jnp.maximum(a, 0.0)).astype(BF16)
        part = jnp.dot(a, w2_ref[j * chunk:(j + 1) * chunk, :], preferred_element_type=F32)
        acc = part if acc is None else acc + part
    o_ref[...] = x + gate_ref[...] * _rms(acc, g3_ref[...])


def _ffn(x2, sc, sh, gate, g2, g3, w1, w2, *, rows_per_seq, per_row_mod):
    n, d = x2.shape
    tm = min(ROW_TILE, n)
    mod_spec = _mod_specs(tm, d, rows_per_seq, per_row_mod)
    const = lambda i: (0, 0)
    row = pl.BlockSpec((tm, d), lambda i: (i, 0))
    return pl.pallas_call(
        functools.partial(_ffn_kernel, chunk=min(1024, w1.shape[1])),
        grid=(n // tm,),
        in_specs=[row, mod_spec, mod_spec, mod_spec, pl.BlockSpec((1, d), const), pl.BlockSpec((1, d), const),
                  pl.BlockSpec(w1.shape, const), pl.BlockSpec(w2.shape, const)],
        out_specs=row,
        out_shape=jax.ShapeDtypeStruct((n, d), F32),
        compiler_params=_params("arbitrary"),
    )(x2, sc, sh, gate, g2, g3, w1, w2)


def _rope_tables(pos):
    half = HEAD_DIM // 2
    inv = jnp.power(ROPE_THETA, -jnp.arange(half, dtype=F32) * 2.0 / HEAD_DIM)
    ang = pos.astype(F32)[:, None] * inv[None, :]
    cos, sin = jnp.cos(ang), jnp.sin(ang)
    reps = LANES // HEAD_DIM
    return jnp.tile(jnp.concatenate([cos, cos], axis=1), (1, reps)), jnp.tile(jnp.concatenate([-sin, sin], axis=1), (1, reps))


def _block_diag_queries(q, width):
    b, s, _ = q.shape
    grp = jnp.arange(DEC_GROUPS)[:, None]
    col = jnp.arange(width)[None, :] // HEAD_DIM
    keep = (grp == col)[None, None]
    out = jnp.where(keep, q[:, :, None, :], jnp.zeros((), q.dtype))
    return out.reshape(b, s * DEC_GROUPS, width)


def _pad_rows(a, rows):
    return jnp.pad(a, ((0, 0), (0, rows - a.shape[1]), (0, 0)))


def kernel(x_prompt, x_sample, cache_a_k, cache_a_v, cache_b_k, cache_b_v, cache_c_k, cache_c_v, cache_c_logf,
           page_table, c_prompt, c_sample, w_mod, b_mod, norm_g, w_in, b_forget, diff_lambda, diff_subln_g,
           w_branch_a, w_branch_b, w_branch_c, w_out, w_ff1, w_ff2):
    bp, t, d = x_prompt.shape
    bs, s_new, _ = x_sample.shape
    depth = w_mod.shape[0]
    n_pool, _, page = cache_a_k.shape[:3]
    n_pages = page_table.shape[1]
    past = n_pages * page
    assert t % MOBA_BLOCK == 0 and past % MOBA_BLOCK == 0 and MOBA_BLOCK % page == 0
    assert s_new <= SUBLANES and (bs * s_new) % SUBLANES == 0

    w_qkv = w_in[:, :, :QKV_WIDTH].astype(BF16)
    w_f = jnp.pad(w_in[:, :, QKV_WIDTH:QKV_WIDTH + N_HEADS], ((0, 0), (0, 0), (0, LANES - N_HEADS))).astype(BF16)
    w_g = w_in[:, :, QKV_WIDTH + N_HEADS:].astype(BF16)
    b_f = jnp.pad(b_forget, ((0, 0), (0, LANES - N_HEADS))).reshape(depth, 1, LANES)
    wa, wb, wc = w_branch_a.astype(BF16), w_branch_b.astype(BF16), w_branch_c.astype(BF16)
    wo, w1, w2 = w_out.astype(BF16), w_ff1.astype(BF16), w_ff2.astype(BF16)

    ca_kt = jnp.transpose(cache_a_k, (0, 1, 3, 4, 5, 2)).reshape(n_pool, depth, WIDTH_A, page)
    ca_v = cache_a_v.reshape(n_pool, depth, page * N_HEADS, 2 * HEAD_DIM)
    cb_kt = jnp.transpose(cache_b_k, (0, 1, 3, 4, 2)).reshape(n_pool, depth, WIDTH_B, page)
    cb_vt = jnp.transpose(cache_b_v, (0, 1, 3, 4, 2)).reshape(n_pool, depth, WIDTH_B, page)
    cc_kt = jnp.transpose(cache_c_k, (0, 1, 3, 4, 2)).reshape(n_pool, depth, WIDTH_C, page)
    cc_vt = jnp.transpose(cache_c_v, (0, 1, 3, 4, 2)).reshape(n_pool, depth, WIDTH_C, page)
    cc_lf = jnp.pad(jnp.swapaxes(cache_c_logf, 2, 3), ((0, 0), (0, 0), (0, SUBLANES - N_HEADS), (0, 0)))

    cos_p, sin_p = _rope_tables(jnp.arange(t))
    cos_s, sin_s = _rope_tables(jnp.tile(past + jnp.arange(s_new), bs))

    mod = _modulation(jnp.concatenate([c_prompt, c_sample], axis=0), w_mod, b_mod)

    xp = x_prompt.reshape(bp * t, d)
    xs = x_sample.reshape(bs * s_new, d)
    blk = min(ATTN_BLOCK, t)
    rows_p, rows_s = [], []
    for l in range(depth):
        lam_init = 0.8 - 0.6 * math.exp(-0.3 * l)
        g = norm_g[l]
        g0, g1, g2, g3 = g[0:1], g[1:2], g[2:3], g[3:4]
        dl, sg = diff_lambda[l], diff_subln_g[l].reshape(1, 2 * HEAD_DIM)

        sh1, sc1, gt1, sh2, sc2, gt2 = [m.reshape(bp, 1, d) for m in jnp.split(mod[l, :bp], 6, axis=-1)]
        q, qbf, kvb, kvf, lf, cum, kmean = _inproj(xp, sc1, sh1, g0, cos_p, sin_p, w_qkv[l], w_f[l], b_f[l],
                                                   rows_per_seq=t, per_row_mod=False)
        q3 = q.reshape(bp, t, -1)
        kvb3 = kvb.reshape(bp, t, -1)
        ya = _attn_a(q3, kvb3, dl, sg, lam_init)
        yb = _attn_b(q3, qbf.reshape(bp, t, WIDTH_B), kvb3, kmean.reshape(bp, t // MOBA_BLOCK, WIDTH_B))
        cum3 = cum.reshape(bp, t, LANES)
        cum_t = jnp.swapaxes(cum3[:, :, :SUBLANES].reshape(bp, t // blk, blk, SUBLANES), 2, 3)
        yc = _attn_c(q3, kvb3, cum3, cum_t)
        xp = _merge(xp, sc1, sh1, gt1, g0, g1, ya.reshape(bp * t, -1), yb.reshape(bp * t, -1), yc.reshape(bp * t, -1),
                    w_g[l], wa[l], wb[l], wc[l], wo[l], rows_per_seq=t, per_row_mod=False)
        xp = _ffn(xp, sc2, sh2, gt2, g2, g3, w1[l], w2[l], rows_per_seq=t, per_row_mod=False)
        rows_p.append((kvf.reshape(bp, t, -1), lf.reshape(bp, t, LANES)))

        sh1, sc1, gt1, sh2, sc2, gt2 = jnp.split(jnp.repeat(mod[l, bp:], s_new, axis=0), 6, axis=-1)
        q, qbf, kvb, kvf, lf, _, _ = _inproj(xs, sc1, sh1, g0, cos_s, sin_s, w_qkv[l], w_f[l], b_f[l],
                                             rows_per_seq=s_new, per_row_mod=True)
        q3 = q.reshape(bs, s_new, -1)
        kvf3 = kvf.reshape(bs, s_new, -1)
        kvp = _pad_rows(kvf3, SUBLANES)
        o = 0
        ka_n = kvp[:, :, o:o + WIDTH_A]; o += WIDTH_A
        va_n = kvp[:, :, o:o + WIDTH_A]; o += WIDTH_A
        kb_n = kvp[:, :, o:o + WIDTH_B]; o += WIDTH_B
        vb_n = kvp[:, :, o:o + WIDTH_B]; o += WIDTH_B
        kc_n = kvp[:, :, o:o + WIDTH_C]; o += WIDTH_C
        vc_n = kvp[:, :, o:o + WIDTH_C]
        qa_bd = _block_diag_queries(q3[:, :, :WIDTH_A], WIDTH_A)
        qb_bd = _block_diag_queries(q3[:, :, WIDTH_A:WIDTH_A + WIDTH_B], WIDTH_B)
        qc_bd = _block_diag_queries(q3[:, :, WIDTH_A + WIDTH_B:], WIDTH_C)
        qbf_bd = _block_diag_queries(qbf.reshape(bs, s_new, WIDTH_B), WIDTH_B)
        lf3 = lf.reshape(bs, s_new, LANES)
        lf_new_t = jnp.pad(jnp.swapaxes(lf3[:, :, :N_HEADS], 1, 2),
                           ((0, 0), (0, SUBLANES - N_HEADS), (0, page - s_new)))
        ya = _dec_a(page_table, qa_bd, ka_n, va_n, ca_kt, ca_v, dl, sg, l, lam_init, s_new)
        yb = _dec_b(page_table, qb_bd, qbf_bd, kb_n, vb_n, cb_kt, cb_vt, l, s_new)
        yc = _dec_c(page_table, qc_bd, kc_n, vc_n, lf_new_t, cc_kt, cc_vt, cc_lf, l, s_new)
        n_s = bs * s_new
        xs = _merge(xs, sc1, sh1, gt1, g0, g1, ya.reshape(n_s, -1), yb.reshape(n_s, -1), yc.reshape(n_s, -1),
                    w_g[l], wa[l], wb[l], wc[l], wo[l], rows_per_seq=s_new, per_row_mod=True)
        xs = _ffn(xs, sc2, sh2, gt2, g2, g3, w1[l], w2[l], rows_per_seq=s_new, per_row_mod=True)
        rows_s.append((kvf3, lf3))

    def assemble(rows, nb, nt):
        kv = jnp.stack([r[0] for r in rows], axis=1)
        lf = jnp.stack([r[1][:, :, :N_HEADS] for r in rows], axis=1)
        o = 0
        a_k = kv[..., o:o + WIDTH_A].reshape(nb, depth, nt, 2, N_HEADS, HEAD_DIM); o += WIDTH_A
        a_v = kv[..., o:o + WIDTH_A].reshape(nb, depth, nt, N_HEADS, 2 * HEAD_DIM); o += WIDTH_A
        b_k = kv[..., o:o + WIDTH_B].reshape(nb, depth, nt, N_HEADS, HEAD_DIM); o += WIDTH_B
        b_v = kv[..., o:o + WIDTH_B].reshape(nb, depth, nt, N_HEADS, HEAD_DIM); o += WIDTH_B
        c_k = kv[..., o:o + WIDTH_C].reshape(nb, depth, nt, N_HEADS, HEAD_DIM); o += WIDTH_C
        c_v = kv[..., o:o + WIDTH_C].reshape(nb, depth, nt, N_HEADS, HEAD_DIM)
        return a_k, a_v, b_k, b_v, c_k, c_v, lf

    return ((xp.reshape(bp, t, d), xs.reshape(bs, s_new, d))
            + assemble(rows_p, bp, t) + assemble(rows_s, bs, s_new))
```

```python
import functools
import math

import jax
import jax.numpy as jnp
from jax import lax
from jax.experimental import pallas as pl
from jax.experimental.pallas import tpu as pltpu

HEAD_DIM = 64
N_HEADS = 4
WIDTH_A = N_HEADS * 2 * HEAD_DIM
WIDTH_B = N_HEADS * HEAD_DIM
WIDTH_C = N_HEADS * HEAD_DIM
QKV_WIDTH = 3 * WIDTH_A + 3 * WIDTH_B + 3 * WIDTH_C
MOBA_BLOCK = 256
MOBA_TOPK = 3
ROPE_THETA = 10000.0
NORM_EPS = 1e-6
LANES = 128
SUBLANES = 8
DEC_GROUPS = 8
ROW_TILE = 256
ATTN_BLOCK = 256
FLASH_LOOKAHEAD = 3
VMEM_LIMIT = 56 * 1024 * 1024

F32 = jnp.float32
BF16 = jnp.bfloat16
HIGHEST = lax.Precision.HIGHEST
LOG2E = math.log2(math.e)
NT_DIMS = (((1,), (1,)), ((), ()))


def _params(*sem):
    return pltpu.CompilerParams(dimension_semantics=sem, vmem_limit_bytes=VMEM_LIMIT)


def _rms(x, g):
    return x * lax.rsqrt(jnp.mean(x * x, axis=-1, keepdims=True) + NORM_EPS) * g


def _sigmoid(x):
    return 1.0 / (1.0 + jnp.exp(-x))


def _nt_dot(a, b, precision=None):
    return lax.dot_general(a, b, NT_DIMS, precision=precision, preferred_element_type=F32)


def _mod_kernel(c_ref, w_ref, b_ref, o_ref):
    c = c_ref[...]
    s = c * _sigmoid(c)
    o_ref[...] = jnp.dot(s, w_ref[...], precision=HIGHEST, preferred_element_type=F32) + b_ref[...]


def _modulation(c_all, w_mod, b_mod):
    depth, d, n6 = w_mod.shape
    m = c_all.shape[0]
    tn = d
    assert n6 % tn == 0
    return pl.pallas_call(
        _mod_kernel,
        grid=(depth, n6 // tn),
        in_specs=[
            pl.BlockSpec((m, d), lambda l, j: (0, 0)),
            pl.BlockSpec((None, d, tn), lambda l, j: (l, 0, j)),
            pl.BlockSpec((None, 1, tn), lambda l, j: (l, 0, j)),
        ],
        out_specs=pl.BlockSpec((None, m, tn), lambda l, j: (l, 0, j)),
        out_shape=jax.ShapeDtypeStruct((depth, m, n6), F32),
        compiler_params=_params("arbitrary", "arbitrary"),
    )(c_all, w_mod, b_mod.reshape(depth, 1, n6))


def _rope_chunks(z, cos, sin):
    lane = lax.broadcasted_iota(jnp.int32, (z.shape[0], LANES), 1)
    first_half = (lane % HEAD_DIM) < (HEAD_DIM // 2)
    out = []
    for c in range(z.shape[1] // LANES):
        zc = z[:, c * LANES:(c + 1) * LANES]
        rot = jnp.where(first_half, pltpu.roll(zc, LANES - HEAD_DIM // 2, 1), pltpu.roll(zc, HEAD_DIM // 2, 1))
        out.append(zc * cos + rot * sin)
    return jnp.concatenate(out, axis=1)


def _log_sigmoid(z):
    return jnp.minimum(z, 0.0) - jnp.log1p(jnp.exp(-jnp.abs(z)))


def _project(x_ref, sc_ref, sh_ref, g_ref, cos_ref, sin_ref, wqkv_ref, wf_ref, bf_ref):
    x = x_ref[...]
    h = _rms(x, g_ref[...]) * (1.0 + sc_ref[...]) + sh_ref[...]
    hb = h.astype(BF16)
    y = jnp.dot(hb, wqkv_ref[...], preferred_element_type=F32)
    cos = cos_ref[...]
    sin = sin_ref[...]
    scale = HEAD_DIM ** -0.5 * LOG2E
    o = 0
    qa = _rope_chunks(y[:, o:o + WIDTH_A], cos, sin) * scale; o += WIDTH_A
    ka = _rope_chunks(y[:, o:o + WIDTH_A], cos, sin); o += WIDTH_A
    va = y[:, o:o + WIDTH_A]; o += WIDTH_A
    qb = _rope_chunks(y[:, o:o + WIDTH_B], cos, sin) * scale; o += WIDTH_B
    kb = _rope_chunks(y[:, o:o + WIDTH_B], cos, sin); o += WIDTH_B
    vb = y[:, o:o + WIDTH_B]; o += WIDTH_B
    qc = y[:, o:o + WIDTH_C] * scale; o += WIDTH_C
    kc = y[:, o:o + WIDTH_C]; o += WIDTH_C
    vc = y[:, o:o + WIDTH_C]
    q = jnp.concatenate([qa, qb, qc], axis=1).astype(BF16)
    lf = _log_sigmoid(jnp.dot(hb, wf_ref[...], preferred_element_type=F32) + bf_ref[...])
    return q, qb, [ka, va, kb, vb, kc, vc], lf


def _inproj_sample_kernel(x_ref, sc_ref, sh_ref, g_ref, cos_ref, sin_ref, wqkv_ref, wf_ref, bf_ref,
                          q_ref, qbf_ref, kvf_ref, lf_ref):
    q, qb, kv, lf = _project(x_ref, sc_ref, sh_ref, g_ref, cos_ref, sin_ref, wqkv_ref, wf_ref, bf_ref)
    q_ref[...] = q
    qbf_ref[...] = qb
    kvf_ref[...] = jnp.concatenate(kv, axis=1)
    lf_ref[...] = lf


def _inproj_prompt_kernel(x_ref, sc_ref, sh_ref, g_ref, cos_ref, sin_ref, wqkv_ref, wf_ref, bf_ref, tri_ref,
                          akt_in, av_in, bkt_in, bvt_in, ckt_in, cvt_in, lft_in,
                          q_ref, qbf_ref, kb_ref, vt_ref, cumrep_ref, cumt_ref, kmean_ref,
                          akt_ref, av_ref, bkt_ref, bvt_ref, ckt_ref, cvt_ref, lft_ref, carry_ref, *, tiles_per_seq):
    i = pl.program_id(0)

    @pl.when(i % tiles_per_seq == 0)
    def _():
        carry_ref[...] = jnp.zeros_like(carry_ref)

    q, qb, kv, lf = _project(x_ref, sc_ref, sh_ref, g_ref, cos_ref, sin_ref, wqkv_ref, wf_ref, bf_ref)
    ka, va, kb, vb, kc, vc = kv
    tm = va.shape[0]
    vw = 2 * HEAD_DIM
    q_ref[...] = q
    qbf_ref[...] = qb
    kb_ref[...] = jnp.concatenate([ka, kb, kc], axis=1).astype(BF16)
    kmean_ref[...] = jnp.mean(kb, axis=0, keepdims=True)
    vbt, vct = vb.T, vc.T
    akt_ref[...] = ka.T
    bkt_ref[...] = kb.T
    bvt_ref[...] = vbt
    ckt_ref[...] = kc.T
    cvt_ref[...] = vct
    for h in range(N_HEADS):
        av_ref[pl.ds(h, tm, stride=N_HEADS), :] = va[:, h * vw:(h + 1) * vw]
    lft_ref[...] = lf.T[:N_HEADS]
    vt_ref[0:WIDTH_A] = va.T.astype(BF16)
    vt_ref[WIDTH_A:WIDTH_A + WIDTH_B] = vbt.astype(BF16)
    vt_ref[WIDTH_A + WIDTH_B:] = vct.astype(BF16)
    cum = jnp.dot(tri_ref[...], lf, precision=HIGHEST, preferred_element_type=F32) + carry_ref[...]
    cum2 = cum * LOG2E
    cumrep_ref[...] = jnp.concatenate([jnp.broadcast_to(cum2[:, h:h + 1], (tm, LANES)) for h in range(N_HEADS)], axis=1)
    cumt_ref[...] = cum2.T[:SUBLANES]
    carry_ref[...] = cum[tm - 1:, :]


def _inproj_specs(n, d, tm, rows_per_seq, per_row_mod):
    tiles_per_seq = max(rows_per_seq // tm, 1)
    if per_row_mod:
        mod_spec = pl.BlockSpec((tm, d), lambda i: (i, 0))
        tab_spec = pl.BlockSpec((tm, LANES), lambda i: (i, 0))
    else:
        mod_spec = pl.BlockSpec((None, 1, d), lambda i: (i // tiles_per_seq, 0, 0))
        tab_spec = pl.BlockSpec((tm, LANES), lambda i: (i % tiles_per_seq, 0))
    const = lambda i: (0, 0)
    return [pl.BlockSpec((tm, d), lambda i: (i, 0)), mod_spec, mod_spec,
            pl.BlockSpec((1, d), const), tab_spec, tab_spec,
            pl.BlockSpec((d, QKV_WIDTH), const), pl.BlockSpec((d, LANES), const), pl.BlockSpec((1, LANES), const)]


def _inproj_sample(x2, sc, sh, g, cos, sin, wqkv, wf, bfg):
    n, d = x2.shape
    tm = min(ROW_TILE, n)
    row = lambda w: pl.BlockSpec((tm, w), lambda i: (i, 0))
    kv_w = 2 * WIDTH_A + 2 * WIDTH_B + 2 * WIDTH_C
    q_w = WIDTH_A + WIDTH_B + WIDTH_C
    return pl.pallas_call(
        _inproj_sample_kernel,
        grid=(n // tm,),
        in_specs=_inproj_specs(n, d, tm, 1, True),
        out_specs=[row(q_w), row(WIDTH_B), row(kv_w), row(LANES)],
        out_shape=[jax.ShapeDtypeStruct((n, q_w), BF16), jax.ShapeDtypeStruct((n, WIDTH_B), F32),
                   jax.ShapeDtypeStruct((n, kv_w), F32), jax.ShapeDtypeStruct((n, LANES), F32)],
        compiler_params=_params("arbitrary"),
    )(x2, sc, sh, g, cos, sin, wqkv, wf, bfg)


def _inproj_prompt(x2, sc, sh, g, cos, sin, wqkv, wf, bfg, caches, layer, *, seq):
    n, d = x2.shape
    tm = ROW_TILE
    tps = seq // tm
    nb = n // seq
    tri = (lax.broadcasted_iota(jnp.int32, (tm, tm), 0) >= lax.broadcasted_iota(jnp.int32, (tm, tm), 1)).astype(F32)
    row = lambda w: pl.BlockSpec((tm, w), lambda i: (i, 0))
    q_w = WIDTH_A + WIDTH_B + WIDTH_C
    feat_major = lambda w: pl.BlockSpec((None, None, w, tm), lambda i: (i // tps, layer, 0, i % tps))
    cache_specs = [feat_major(WIDTH_A),
                   pl.BlockSpec((None, None, tm * N_HEADS, 2 * HEAD_DIM), lambda i: (i // tps, layer, i % tps, 0)),
                   feat_major(WIDTH_B), feat_major(WIDTH_B), feat_major(WIDTH_C), feat_major(WIDTH_C),
                   feat_major(N_HEADS)]
    n_in, n_out = 10, 7
    outs = pl.pallas_call(
        functools.partial(_inproj_prompt_kernel, tiles_per_seq=tps),
        grid=(n // tm,),
        in_specs=_inproj_specs(n, d, tm, seq, False) + [pl.BlockSpec((tm, tm), lambda i: (0, 0))]
        + [pl.BlockSpec(memory_space=pl.ANY)] * len(caches),
        out_specs=[row(q_w), row(WIDTH_B), row(q_w),
                   pl.BlockSpec((None, None, q_w, tm), lambda i: (i // tps, i % tps, 0, 0)),
                   row(N_HEADS * LANES),
                   pl.BlockSpec((None, None, SUBLANES, tm), lambda i: (i // tps, i % tps, 0, 0)),
                   pl.BlockSpec((None, 1, WIDTH_B), lambda i: (i, 0, 0))] + cache_specs,
        out_shape=[jax.ShapeDtypeStruct((n, q_w), BF16), jax.ShapeDtypeStruct((n, WIDTH_B), F32),
                   jax.ShapeDtypeStruct((n, q_w), BF16), jax.ShapeDtypeStruct((nb, tps, q_w, tm), BF16),
                   jax.ShapeDtypeStruct((n, N_HEADS * LANES), F32),
                   jax.ShapeDtypeStruct((nb, tps, SUBLANES, tm), F32),
                   jax.ShapeDtypeStruct((n // tm, 1, WIDTH_B), F32)]
        + [jax.ShapeDtypeStruct(c.shape, c.dtype) for c in caches],
        input_output_aliases={n_in + k: n_out + k for k in range(len(caches))},
        scratch_shapes=[pltpu.VMEM((1, LANES), F32)],
        compiler_params=_params("arbitrary"),
    )(x2, sc, sh, g, cos, sin, wqkv, wf, bfg, tri, *caches)
    outs = list(outs)
    return tuple(outs[:n_out]) + (outs[n_out:],)


def _head_lane_mask(width, head):
    lane = lax.broadcasted_iota(jnp.int32, (1, width), 1)
    return (lane // HEAD_DIM) == head


def _store_head_queries(q_ref, qm_ref, n_states):
    for g in range(n_states):
        col = (g // 2) * LANES
        qg = q_ref[:, col:col + LANES]
        qm_ref[g] = jnp.where(_head_lane_mask(LANES, g % 2), qg, jnp.zeros_like(qg))


def _flash_states(n_states, qi, block, score_fn, value_fn, m_ref, l_ref, acc_ref):
    key = lax.broadcasted_iota(jnp.int32, (block, block), 0)
    qry = lax.broadcasted_iota(jnp.int32, (block, block), 1)

    def sweep(j, first):
        start = pl.multiple_of(j * block, block)
        pending = [score_fn(st, j, start) for st in range(min(FLASH_LOOKAHEAD, n_states))]
        for st in range(n_states):
            if st + FLASH_LOOKAHEAD < n_states:
                pending.append(score_fn(st + FLASH_LOOKAHEAD, j, start))
            s = pending[st]
            if first:
                s = jnp.where(qry >= key, s, -jnp.inf)
                m_new = jnp.max(s, axis=0, keepdims=True)
                p = jnp.exp2(s - m_new)
                l_ref[st] = jnp.sum(p, axis=0, keepdims=True)
                acc_ref[st] = jnp.dot(value_fn(st, j), p.astype(BF16), preferred_element_type=F32)
            else:
                m_old = m_ref[st]
                m_new = jnp.maximum(m_old, jnp.max(s, axis=0, keepdims=True))
                alpha = jnp.exp2(m_old - m_new)
                p = jnp.exp2(s - m_new)
                l_ref[st] = alpha * l_ref[st] + jnp.sum(p, axis=0, keepdims=True)
                acc_ref[st] = alpha * acc_ref[st] + jnp.dot(value_fn(st, j), p.astype(BF16),
                                                            preferred_element_type=F32)
            m_ref[st] = m_new

    sweep(qi, True)

    def body(j, carry):
        sweep(j, False)
        return carry

    lax.fori_loop(0, qi, body, 0)


def _flash_scratch(n_states, block):
    return [pltpu.VMEM((n_states, block, LANES), BF16),
            pltpu.VMEM((n_states, 1, block), F32),
            pltpu.VMEM((n_states, 1, block), F32),
            pltpu.VMEM((n_states, LANES, block), F32)]


def _key_spec(t, width, col_block):
    return pl.BlockSpec((None, t, width), lambda bi, qi: (bi, 0, col_block))


def _value_spec(nk, width, blk, row_block):
    return pl.BlockSpec((None, nk, width, blk), lambda bi, qi: (bi, 0, row_block, 0))


def _attn_a_kernel(q_ref, k_ref, vt_ref, lam_ref, g_ref, o_ref, qm_ref, m_ref, l_ref, acc_ref, *, lam_init, block):
    qi = pl.program_id(1)
    n_states = 2 * N_HEADS
    vw = 2 * HEAD_DIM
    _store_head_queries(q_ref, qm_ref, n_states)

    def score(g, j, start):
        kcol = (g // 2) * LANES
        return _nt_dot(k_ref[pl.ds(start, block), kcol:kcol + LANES], qm_ref[g])

    def value(g, j):
        h = g % N_HEADS
        return vt_ref[j, h * vw:(h + 1) * vw, :]

    _flash_states(n_states, qi, block, score, value, m_ref, l_ref, acc_ref)
    lv = lam_ref[...]
    lam = (jnp.exp(jnp.sum(lv[0:1] * lv[1:2], axis=-1, keepdims=True))
           - jnp.exp(jnp.sum(lv[2:3] * lv[3:4], axis=-1, keepdims=True)) + lam_init)
    for h in range(N_HEADS):
        y_t = acc_ref[h] / l_ref[h] - lam * (acc_ref[N_HEADS + h] / l_ref[N_HEADS + h])
        y = _rms(y_t.T, g_ref[...]) * (1.0 - lam_init)
        o_ref[:, h * vw:(h + 1) * vw] = y.astype(o_ref.dtype)


def _attn_a(q3, k3, vt4, diff_lambda, subln_g, lam_init):
    b, t, _ = q3.shape
    blk = ATTN_BLOCK
    nk = t // blk
    return pl.pallas_call(
        functools.partial(_attn_a_kernel, lam_init=lam_init, block=blk),
        grid=(b, nk),
        in_specs=[
            pl.BlockSpec((None, blk, WIDTH_A), lambda bi, qi: (bi, qi, 0)),
            _key_spec(t, WIDTH_A, 0),
            _value_spec(nk, WIDTH_A, blk, 0),
            pl.BlockSpec((4, HEAD_DIM), lambda bi, qi: (0, 0)),
            pl.BlockSpec((1, 2 * HEAD_DIM), lambda bi, qi: (0, 0)),
        ],
        out_specs=pl.BlockSpec((None, blk, WIDTH_A), lambda bi, qi: (bi, qi, 0)),
        out_shape=jax.ShapeDtypeStruct((b, t, WIDTH_A), BF16),
        scratch_shapes=_flash_scratch(2 * N_HEADS, blk),
        compiler_params=_params("arbitrary", "arbitrary"),
    )(q3, k3, vt4, diff_lambda, subln_g)


def _topk_block_mask(gate, valid, axis):
    nb = gate.shape[axis]
    blk = lax.broadcasted_iota(jnp.int32, gate.shape, axis)
    g = jnp.where(valid, gate, -jnp.inf)
    rank = jnp.zeros(gate.shape, jnp.int32)
    for m in range(nb):
        gm = g[m:m + 1, :] if axis == 0 else g[:, m:m + 1]
        ahead = (gm > g) | ((gm == g) & (m < blk))
        rank = rank + ahead.astype(jnp.int32)
    return (valid & (rank < MOBA_TOPK)).astype(F32)


def _store_pair_outputs(o_ref, l_ref, acc_ref):
    for pair in range(N_HEADS // 2):
        even = acc_ref[2 * pair] / l_ref[2 * pair]
        odd = acc_ref[2 * pair + 1] / l_ref[2 * pair + 1]
        y_t = jnp.concatenate([even[:HEAD_DIM], odd[HEAD_DIM:]], axis=0)
        o_ref[:, pair * LANES:(pair + 1) * LANES] = y_t.T.astype(o_ref.dtype)


def _attn_b_kernel(q_ref, qf_ref, k_ref, vt_ref, kmean_ref, o_ref, qm_ref, m_ref, l_ref, acc_ref, sel_ref, *, block):
    qi = pl.program_id(1)
    nb = kmean_ref.shape[0]
    kmean = kmean_ref[...]
    qf = qf_ref[...]
    blk_id = lax.broadcasted_iota(jnp.int32, (nb, block), 0)
    valid = blk_id < qi
    _store_head_queries(q_ref, qm_ref, N_HEADS)
    for h in range(N_HEADS):
        kmean_h = jnp.where(_head_lane_mask(WIDTH_B, h), kmean, 0.0)
        gate = _nt_dot(kmean_h, qf, precision=HIGHEST)
        sel_ref[h] = jnp.where(blk_id == qi, 1.0, _topk_block_mask(gate, valid, 0))

    def score(h, j, start):
        kcol = (h // 2) * LANES
        s = _nt_dot(k_ref[pl.ds(start, block), kcol:kcol + LANES], qm_ref[h])
        return jnp.where(sel_ref[h, pl.ds(j, 1), :] > 0.5, s, -jnp.inf)

    def value(h, j):
        vrow = (h // 2) * LANES
        return vt_ref[j, vrow:vrow + LANES, :]

    _flash_states(N_HEADS, qi, block, score, value, m_ref, l_ref, acc_ref)
    _store_pair_outputs(o_ref, l_ref, acc_ref)


def _attn_b(q3, qbf3, k3, vt4, kmean3):
    b, t, _ = q3.shape
    blk = MOBA_BLOCK
    nb = t // blk
    return pl.pallas_call(
        functools.partial(_attn_b_kernel, block=blk),
        grid=(b, nb),
        in_specs=[
            pl.BlockSpec((None, blk, WIDTH_B), lambda bi, qi: (bi, qi, WIDTH_A // WIDTH_B)),
            pl.BlockSpec((None, blk, WIDTH_B), lambda bi, qi: (bi, qi, 0)),
            _key_spec(t, WIDTH_B, WIDTH_A // WIDTH_B),
            _value_spec(nb, WIDTH_B, blk, WIDTH_A // WIDTH_B),
            pl.BlockSpec((None, nb, WIDTH_B), lambda bi, qi: (bi, 0, 0)),
        ],
        out_specs=pl.BlockSpec((None, blk, WIDTH_B), lambda bi, qi: (bi, qi, 0)),
        out_shape=jax.ShapeDtypeStruct((b, t, WIDTH_B), BF16),
        scratch_shapes=_flash_scratch(N_HEADS, blk) + [pltpu.VMEM((N_HEADS, nb, blk), F32)],
        compiler_params=_params("arbitrary", "arbitrary"),
    )(q3, qbf3, k3, vt4, kmean3)


def _attn_c_kernel(q_ref, k_ref, vt_ref, fq_ref, fk_ref, o_ref, qm_ref, m_ref, l_ref, acc_ref, *, block):
    qi = pl.program_id(1)
    _store_head_queries(q_ref, qm_ref, N_HEADS)

    def score(h, j, start):
        kcol = (h // 2) * LANES
        s = _nt_dot(k_ref[pl.ds(start, block), kcol:kcol + LANES], qm_ref[h])
        f_key = fk_ref[pl.ds(start, block), h * LANES:(h + 1) * LANES]
        return s + (fq_ref[h:h + 1, :] - jnp.concatenate([f_key] * (block // LANES), axis=1))

    def value(h, j):
        vrow = (h // 2) * LANES
        return vt_ref[j, vrow:vrow + LANES, :]

    _flash_states(N_HEADS, qi, block, score, value, m_ref, l_ref, acc_ref)
    _store_pair_outputs(o_ref, l_ref, acc_ref)


def _attn_c(q3, k3, vt4, cum_rep3, cum_t):
    b, t, _ = q3.shape
    blk = ATTN_BLOCK
    nk = t // blk
    off = (WIDTH_A + WIDTH_B) // WIDTH_C
    return pl.pallas_call(
        functools.partial(_attn_c_kernel, block=blk),
        grid=(b, nk),
        in_specs=[
            pl.BlockSpec((None, blk, WIDTH_C), lambda bi, qi: (bi, qi, off)),
            _key_spec(t, WIDTH_C, off),
            _value_spec(nk, WIDTH_C, blk, off),
            pl.BlockSpec((None, None, SUBLANES, blk), lambda bi, qi: (bi, qi, 0, 0)),
            pl.BlockSpec((None, t, N_HEADS * LANES), lambda bi, qi: (bi, 0, 0)),
        ],
        out_specs=pl.BlockSpec((None, blk, WIDTH_C), lambda bi, qi: (bi, qi, 0)),
        out_shape=jax.ShapeDtypeStruct((b, t, WIDTH_C), BF16),
        scratch_shapes=_flash_scratch(N_HEADS, blk),
        compiler_params=_params("arbitrary", "arbitrary"),
    )(q3, k3, vt4, cum_t, cum_rep3)


def _tail_page(new, page):
    return jnp.concatenate([new, jnp.zeros((page - new.shape[0], new.shape[1]), new.dtype)], axis=0)


def _tail_mask(rows, page, n_new):
    tok = lax.broadcasted_iota(jnp.int32, (rows, page), 0) // DEC_GROUPS
    key = lax.broadcasted_iota(jnp.int32, (rows, page), 1)
    return (key < n_new) & (key <= tok)


def _sum_list(xs):
    return functools.reduce(lambda a, b: a + b, xs)


def _dec_softmax(s_pages):
    m = functools.reduce(jnp.maximum, [jnp.max(s, axis=-1, keepdims=True) for s in s_pages])
    e_pages = [jnp.exp2(s - m) for s in s_pages]
    l = _sum_list([jnp.sum(e, axis=-1, keepdims=True) for e in e_pages])
    return [e.astype(BF16) for e in e_pages], l


def _collapse_groups(z, n_new):
    return jnp.sum(z.reshape(n_new, DEC_GROUPS, z.shape[1]), axis=1)


def _page_groups(rest, n_groups, n_pages):
    return [rest[i * n_pages:(i + 1) * n_pages] for i in range(n_groups)], rest[n_groups * n_pages]


def _dec_a_kernel(pt_ref, q_ref, kn_ref, vn_ref, lam_ref, g_ref, *rest, n_pages, n_new, lam_init, bpb):
    groups, o_ref = _page_groups(rest, 2 * bpb, n_pages)
    page = groups[0][0].shape[1]
    vw = 2 * HEAD_DIM
    lv = lam_ref[...]
    lam = (jnp.exp(jnp.sum(lv[0:1] * lv[1:2], axis=-1, keepdims=True))
           - jnp.exp(jnp.sum(lv[2:3] * lv[3:4], axis=-1, keepdims=True)) + lam_init)
    g = g_ref[...]
    for u in range(bpb):
        kt_refs, v_refs = groups[u], groups[bpb + u]
        q = q_ref[u]
        rows = q.shape[0]
        s_pages = [jnp.dot(q, r[...].astype(BF16), preferred_element_type=F32) for r in kt_refs]
        s_tail = _nt_dot(q, _tail_page(kn_ref[u], page).astype(BF16))
        s_pages.append(jnp.where(_tail_mask(rows, page, n_new), s_tail, -jnp.inf))
        e_pages, l = _dec_softmax(s_pages)
        v_tail = _tail_page(vn_ref[u], page).astype(BF16)
        acc_heads = []
        for h in range(N_HEADS):
            parts = [jnp.dot(e_pages[p], v_refs[p][pl.ds(h, page, stride=N_HEADS), :].astype(BF16),
                             preferred_element_type=F32) for p in range(n_pages)]
            parts.append(jnp.dot(e_pages[n_pages], v_tail[:, h * vw:(h + 1) * vw], preferred_element_type=F32))
            acc_heads.append(_sum_list(parts))
        acc = jnp.concatenate(acc_heads, axis=1)
        grp = lax.broadcasted_iota(jnp.int32, acc.shape, 0) % DEC_GROUPS
        vhead = lax.broadcasted_iota(jnp.int32, acc.shape, 1) // vw
        coef = jnp.where(grp < N_HEADS, 1.0, -lam)
        z = jnp.where((grp % N_HEADS) == vhead, (acc / l) * coef, 0.0)
        y = _collapse_groups(z, n_new)
        y = jnp.concatenate([_rms(y[:, h * vw:(h + 1) * vw], g) for h in range(N_HEADS)], axis=1) * (1.0 - lam_init)
        o_ref[u] = y.astype(o_ref.dtype)


def _page_specs(n_pages, layer, rows, cols, bpb, u):
    return [pl.BlockSpec((None, None, rows, cols), lambda bi, pt, p=p: (pt[bi * bpb + u, p], layer, 0, 0))
            for p in range(n_pages)]


def _batch_spec(bpb, rows, cols):
    return pl.BlockSpec((bpb, rows, cols), lambda bi, pt: (bi, 0, 0))


def _dec_a(page_table, qbd, k_new, v_new, cache_kt, cache_v, diff_lambda, subln_g, layer, lam_init, n_new, bpb):
    nb, rows, _ = qbd.shape
    n_pages = page_table.shape[1]
    page = cache_kt.shape[3]
    pad = k_new.shape[1]
    specs = []
    for u in range(bpb):
        specs += _page_specs(n_pages, layer, WIDTH_A, page, bpb, u)
    for u in range(bpb):
        specs += _page_specs(n_pages, layer, page * N_HEADS, 2 * HEAD_DIM, bpb, u)
    grid_spec = pltpu.PrefetchScalarGridSpec(
        num_scalar_prefetch=1,
        grid=(nb // bpb,),
        in_specs=[
            _batch_spec(bpb, rows, WIDTH_A), _batch_spec(bpb, pad, WIDTH_A), _batch_spec(bpb, pad, WIDTH_A),
            pl.BlockSpec((4, HEAD_DIM), lambda bi, pt: (0, 0)),
            pl.BlockSpec((1, 2 * HEAD_DIM), lambda bi, pt: (0, 0)),
        ] + specs,
        out_specs=_batch_spec(bpb, n_new, WIDTH_A),
    )
    return pl.pallas_call(
        functools.partial(_dec_a_kernel, n_pages=n_pages, n_new=n_new, lam_init=lam_init, bpb=bpb),
        grid_spec=grid_spec,
        out_shape=jax.ShapeDtypeStruct((nb, n_new, WIDTH_A), BF16),
        compiler_params=_params("arbitrary"),
    )(page_table, qbd, k_new, v_new, diff_lambda, subln_g,
      *([cache_kt] * (n_pages * bpb)), *([cache_v] * (n_pages * bpb)))


def _dec_pv_t(e_pages, vt_refs, v_new, page):
    n_pages = len(vt_refs)
    parts = [_nt_dot(e_pages[p], vt_refs[p][...].astype(BF16)) for p in range(n_pages)]
    parts.append(jnp.dot(e_pages[n_pages], _tail_page(v_new, page).astype(BF16), preferred_element_type=F32))
    return _sum_list(parts)


def _dec_head_outputs(acc, l, n_new):
    grp = lax.broadcasted_iota(jnp.int32, acc.shape, 0) % DEC_GROUPS
    vhead = lax.broadcasted_iota(jnp.int32, acc.shape, 1) // HEAD_DIM
    return _collapse_groups(jnp.where(grp == vhead, acc / l, 0.0), n_new)


def _dec_b_kernel(pt_ref, q_ref, qf_ref, kn_ref, vn_ref, *rest, n_pages, n_new, bpb):
    groups, o_ref = _page_groups(rest, 2 * bpb, n_pages)
    page = groups[0][0].shape[1]
    pages_per_block = MOBA_BLOCK // page
    n_blocks = n_pages // pages_per_block
    lane_id = lax.broadcasted_iota(jnp.int32, (WIDTH_B, n_blocks), 1)
    for u in range(bpb):
        kt_refs, vt_refs = groups[u], groups[bpb + u]
        q = q_ref[u]
        rows = q.shape[0]
        kt_f32 = [r[...] for r in kt_refs]
        kmean_t = jnp.zeros((WIDTH_B, n_blocks), F32)
        for j in range(n_blocks):
            tot = _sum_list(kt_f32[j * pages_per_block:(j + 1) * pages_per_block])
            col = jnp.sum(tot, axis=1, keepdims=True) * (1.0 / MOBA_BLOCK)
            kmean_t = jnp.where(lane_id == j, col, kmean_t)
        gate = jnp.dot(qf_ref[u], kmean_t, precision=HIGHEST, preferred_element_type=F32)
        sel = _topk_block_mask(gate, jnp.full(gate.shape, True), 1)
        s_pages = []
        for p in range(n_pages):
            s = jnp.dot(q, kt_f32[p].astype(BF16), preferred_element_type=F32)
            j = p // pages_per_block
            s_pages.append(jnp.where(sel[:, j:j + 1] > 0.5, s, -jnp.inf))
        s_tail = _nt_dot(q, _tail_page(kn_ref[u], page).astype(BF16))
        s_pages.append(jnp.where(_tail_mask(rows, page, n_new), s_tail, -jnp.inf))
        e_pages, l = _dec_softmax(s_pages)
        acc = _dec_pv_t(e_pages, vt_refs, vn_ref[u], page)
        o_ref[u] = _dec_head_outputs(acc, l, n_new).astype(o_ref.dtype)


def _dec_b(page_table, qbd, qbd_f32, k_new, v_new, cache_kt, cache_vt, layer, n_new, bpb):
    nb, rows, _ = qbd.shape
    n_pages = page_table.shape[1]
    page = cache_kt.shape[3]
    pad = k_new.shape[1]
    specs = []
    for u in range(2 * bpb):
        specs += _page_specs(n_pages, layer, WIDTH_B, page, bpb, u % bpb)
    grid_spec = pltpu.PrefetchScalarGridSpec(
        num_scalar_prefetch=1,
        grid=(nb // bpb,),
        in_specs=[_batch_spec(bpb, rows, WIDTH_B), _batch_spec(bpb, rows, WIDTH_B),
                  _batch_spec(bpb, pad, WIDTH_B), _batch_spec(bpb, pad, WIDTH_B)] + specs,
        out_specs=_batch_spec(bpb, n_new, WIDTH_B),
    )
    return pl.pallas_call(
        functools.partial(_dec_b_kernel, n_pages=n_pages, n_new=n_new, bpb=bpb),
        grid_spec=grid_spec,
        out_shape=jax.ShapeDtypeStruct((nb, n_new, WIDTH_B), BF16),
        compiler_params=_params("arbitrary"),
    )(page_table, qbd, qbd_f32, k_new, v_new,
      *([cache_kt] * (n_pages * bpb)), *([cache_vt] * (n_pages * bpb)))


def _split3(x):
    hi = x.astype(BF16)
    r1 = x - hi.astype(F32)
    mid = r1.astype(BF16)
    lo = (r1 - mid.astype(F32)).astype(BF16)
    return hi, mid, lo


def _dec_c_kernel(pt_ref, q_ref, kn_ref, vn_ref, lfn_ref, tri_ref, prefix_ref, *rest, n_pages, n_new, bpb):
    groups, o_ref = _page_groups(rest, 3 * bpb, n_pages)
    page = groups[0][0].shape[1]
    tri = tri_ref[...]
    for u in range(bpb):
        kt_refs, vt_refs, f_refs = groups[u], groups[bpb + u], groups[2 * bpb + u]
        q = q_ref[u]
        rows = q.shape[0]
        slots = prefix_ref.shape[0] // SUBLANES
        lf_all = jnp.concatenate([r[...] for r in f_refs] + [lfn_ref[u]]
                                 + [jnp.zeros(((slots - n_pages - 1) * SUBLANES, page), F32)], axis=0)
        both = _sum_list([jnp.dot(part, tri, preferred_element_type=F32) for part in _split3(lf_all)])
        within, totals = both[:, :page], both[:, page:]
        earlier = _sum_list([jnp.dot(prefix_ref[...], part, preferred_element_type=F32) for part in _split3(totals)])
        cum_all = within + earlier
        cum_pages = [cum_all[p * SUBLANES:(p + 1) * SUBLANES] for p in range(n_pages + 1)]
        cum_new = jnp.concatenate([cum_pages[-1]] * n_new, axis=0)
        tok = lax.broadcasted_iota(jnp.int32, (rows, page), 0) // DEC_GROUPS
        key = lax.broadcasted_iota(jnp.int32, (rows, page), 1)
        f_query = jnp.sum(jnp.where(key == tok, cum_new, 0.0), axis=-1, keepdims=True)
        s_pages = []
        for p in range(n_pages + 1):
            if p < n_pages:
                s = jnp.dot(q, kt_refs[p][...].astype(BF16), preferred_element_type=F32)
            else:
                s = _nt_dot(q, _tail_page(kn_ref[u], page).astype(BF16))
            f_key = jnp.concatenate([cum_pages[p]] * n_new, axis=0)
            s_pages.append(s + (f_query - f_key) * LOG2E)
        s_pages[-1] = jnp.where(_tail_mask(rows, page, n_new), s_pages[-1], -jnp.inf)
        e_pages, l = _dec_softmax(s_pages)
        acc = _dec_pv_t(e_pages, vt_refs, vn_ref[u], page)
        o_ref[u] = _dec_head_outputs(acc, l, n_new).astype(o_ref.dtype)


def _dec_c(page_table, qbd, k_new, v_new, lf_new_t, cache_kt, cache_vt, cache_lf_t, layer, n_new, bpb):
    nb, rows, _ = qbd.shape
    n_pages = page_table.shape[1]
    page = cache_kt.shape[3]
    pad = k_new.shape[1]
    upper = lax.broadcasted_iota(jnp.int32, (page, page), 0) <= lax.broadcasted_iota(jnp.int32, (page, page), 1)
    tri = jnp.concatenate([upper, jnp.full((page, page), True)], axis=1).astype(BF16)
    n_rows = 2 * LANES
    assert (n_pages + 1) * SUBLANES <= n_rows
    r = lax.broadcasted_iota(jnp.int32, (n_rows, n_rows), 0)
    r2 = lax.broadcasted_iota(jnp.int32, (n_rows, n_rows), 1)
    prefix = ((r % SUBLANES == r2 % SUBLANES) & (r2 // SUBLANES < r // SUBLANES)).astype(BF16)
    specs = []
    for u in range(2 * bpb):
        specs += _page_specs(n_pages, layer, WIDTH_C, page, bpb, u % bpb)
    for u in range(bpb):
        specs += _page_specs(n_pages, layer, SUBLANES, page, bpb, u)
    grid_spec = pltpu.PrefetchScalarGridSpec(
        num_scalar_prefetch=1,
        grid=(nb // bpb,),
        in_specs=[_batch_spec(bpb, rows, WIDTH_C), _batch_spec(bpb, pad, WIDTH_C), _batch_spec(bpb, pad, WIDTH_C),
                  _batch_spec(bpb, SUBLANES, page),
                  pl.BlockSpec((page, 2 * page), lambda bi, pt: (0, 0)),
                  pl.BlockSpec((n_rows, n_rows), lambda bi, pt: (0, 0))] + specs,
        out_specs=_batch_spec(bpb, n_new, WIDTH_C),
    )
    return pl.pallas_call(
        functools.partial(_dec_c_kernel, n_pages=n_pages, n_new=n_new, bpb=bpb),
        grid_spec=grid_spec,
        out_shape=jax.ShapeDtypeStruct((nb, n_new, WIDTH_C), BF16),
        compiler_params=_params("arbitrary"),
    )(page_table, qbd, k_new, v_new, lf_new_t, tri, prefix,
      *([cache_kt] * (n_pages * bpb)), *([cache_vt] * (n_pages * bpb)), *([cache_lf_t] * (n_pages * bpb)))


def _merge_kernel(x_ref, sc_ref, sh_ref, gate_ref, g0_ref, g1_ref, ya_ref, yb_ref, yc_ref,
                  wg_ref, wa_ref, wb_ref, wc_ref, wo_ref, o_ref):
    x = x_ref[...]
    d = x.shape[1]
    h = _rms(x, g0_ref[...]) * (1.0 + sc_ref[...]) + sh_ref[...]
    gl = jnp.dot(h.astype(BF16), wg_ref[...], preferred_element_type=F32)
    merged = (_sigmoid(gl[:, :d]) * jnp.dot(ya_ref[...], wa_ref[...], preferred_element_type=F32)
              + _sigmoid(gl[:, d:2 * d]) * jnp.dot(yb_ref[...], wb_ref[...], preferred_element_type=F32)
              + _sigmoid(gl[:, 2 * d:]) * jnp.dot(yc_ref[...], wc_ref[...], preferred_element_type=F32))
    o = jnp.dot(merged.astype(BF16), wo_ref[...], preferred_element_type=F32)
    o_ref[...] = x + gate_ref[...] * _rms(o, g1_ref[...])


def _mod_specs(tm, d, rows_per_seq, per_row_mod):
    if per_row_mod:
        return pl.BlockSpec((tm, d), lambda i: (i, 0))
    tiles_per_seq = rows_per_seq // tm
    return pl.BlockSpec((None, 1, d), lambda i: (i // tiles_per_seq, 0, 0))


def _merge(x2, sc, sh, gate, g0, g1, ya, yb, yc, wg, wa, wb, wc, wo, *, rows_per_seq, per_row_mod):
    n, d = x2.shape
    tm = min(ROW_TILE, n)
    mod_spec = _mod_specs(tm, d, rows_per_seq, per_row_mod)
    const = lambda i: (0, 0)
    row = lambda w: pl.BlockSpec((tm, w), lambda i: (i, 0))
    full = lambda a: pl.BlockSpec(a.shape, const)
    return pl.pallas_call(
        _merge_kernel,
        grid=(n // tm,),
        in_specs=[row(d), mod_spec, mod_spec, mod_spec, pl.BlockSpec((1, d), const), pl.BlockSpec((1, d), const),
                  row(WIDTH_A), row(WIDTH_B), row(WIDTH_C), full(wg), full(wa), full(wb), full(wc), full(wo)],
        out_specs=row(d),
        out_shape=jax.ShapeDtypeStruct((n, d), F32),
        compiler_params=_params("arbitrary"),
    )(x2, sc, sh, gate, g0, g1, ya, yb, yc, wg, wa, wb, wc, wo)


def _ffn_kernel(x_ref, sc_ref, sh_ref, gate_ref, g2_ref, g3_ref, w1_ref, w2_ref, o_ref, *, chunk):
    x = x_ref[...]
    h = (_rms(x, g2_ref[...]) * (1.0 + sc_ref[...]) + sh_ref[...]).astype(BF16)
    d_ff = w1_ref.shape[1]
    acc = None
    for j in range(d_ff // chunk):
        a = jnp.dot(h, w1_ref[:, j * chunk:(j + 1) * chunk], preferred_element_type=F32)
        a = jnp.square(jnp.maximum(a, 0.0)).astype(BF16)
        part = jnp.dot(a, w2_ref[j * chunk:(j + 1) * chunk, :], preferred_element_type=F32)
        acc = part if acc is None else acc + part
    o_ref[...] = x + gate_ref[...] * _rms(acc, g3_ref[...])


def _ffn(x2, sc, sh, gate, g2, g3, w1, w2, *, rows_per_seq, per_row_mod):
    n, d = x2.shape
    tm = min(ROW_TILE, n)
    mod_spec = _mod_specs(tm, d, rows_per_seq, per_row_mod)
    const = lambda i: (0, 0)
    row = pl.BlockSpec((tm, d), lambda i: (i, 0))
    return pl.pallas_call(
        functools.partial(_ffn_kernel, chunk=min(1024, w1.shape[1])),
        grid=(n // tm,),
        in_specs=[row, mod_spec, mod_spec, mod_spec, pl.BlockSpec((1, d), const), pl.BlockSpec((1, d), const),
                  pl.BlockSpec(w1.shape, const), pl.BlockSpec(w2.shape, const)],
        out_specs=row,
        out_shape=jax.ShapeDtypeStruct((n, d), F32),
        compiler_params=_params("arbitrary"),
    )(x2, sc, sh, gate, g2, g3, w1, w2)


def _rope_tables(pos):
    half = HEAD_DIM // 2
    inv = jnp.power(ROPE_THETA, -jnp.arange(half, dtype=F32) * 2.0 / HEAD_DIM)
    ang = pos.astype(F32)[:, None] * inv[None, :]
    cos, sin = jnp.cos(ang), jnp.sin(ang)
    reps = LANES // HEAD_DIM
    return jnp.tile(jnp.concatenate([cos, cos], axis=1), (1, reps)), jnp.tile(jnp.concatenate([-sin, sin], axis=1), (1, reps))


def _block_diag_queries(q, width):
    b, s, _ = q.shape
    grp = jnp.arange(DEC_GROUPS)[:, None]
    col = jnp.arange(width)[None, :] // HEAD_DIM
    keep = (grp == col)[None, None]
    out = jnp.where(keep, q[:, :, None, :], jnp.zeros((), q.dtype))
    return out.reshape(b, s * DEC_GROUPS, width)


def _pad_rows(a, rows):
    return jnp.pad(a, ((0, 0), (0, rows - a.shape[1]), (0, 0)))


def kernel(x_prompt, x_sample, cache_a_k, cache_a_v, cache_b_k, cache_b_v, cache_c_k, cache_c_v, cache_c_logf,
           page_table, c_prompt, c_sample, w_mod, b_mod, norm_g, w_in, b_forget, diff_lambda, diff_subln_g,
           w_branch_a, w_branch_b, w_branch_c, w_out, w_ff1, w_ff2):
    bp, t, d = x_prompt.shape
    bs, s_new, _ = x_sample.shape
    depth = w_mod.shape[0]
    n_pool, _, page = cache_a_k.shape[:3]
    n_pages = page_table.shape[1]
    past = n_pages * page
    assert t % MOBA_BLOCK == 0 and past % MOBA_BLOCK == 0 and MOBA_BLOCK % page == 0
    assert s_new <= SUBLANES and (bs * s_new) % SUBLANES == 0 and ATTN_BLOCK == ROW_TILE == MOBA_BLOCK
    dec_bpb = 2 if bs % 2 == 0 else 1

    w_qkv = w_in[:, :, :QKV_WIDTH].astype(BF16)
    w_f = jnp.pad(w_in[:, :, QKV_WIDTH:QKV_WIDTH + N_HEADS], ((0, 0), (0, 0), (0, LANES - N_HEADS))).astype(BF16)
    w_g = w_in[:, :, QKV_WIDTH + N_HEADS:].astype(BF16)
    b_f = jnp.pad(b_forget, ((0, 0), (0, LANES - N_HEADS))).reshape(depth, 1, LANES)
    wa, wb, wc = w_branch_a.astype(BF16), w_branch_b.astype(BF16), w_branch_c.astype(BF16)
    wo, w1, w2 = w_out.astype(BF16), w_ff1.astype(BF16), w_ff2.astype(BF16)

    ca_kt = jnp.transpose(cache_a_k, (0, 1, 3, 4, 5, 2)).reshape(n_pool, depth, WIDTH_A, page)
    ca_v = cache_a_v.reshape(n_pool, depth, page * N_HEADS, 2 * HEAD_DIM)
    cb_kt = jnp.transpose(cache_b_k, (0, 1, 3, 4, 2)).reshape(n_pool, depth, WIDTH_B, page)
    cb_vt = jnp.transpose(cache_b_v, (0, 1, 3, 4, 2)).reshape(n_pool, depth, WIDTH_B, page)
    cc_kt = jnp.transpose(cache_c_k, (0, 1, 3, 4, 2)).reshape(n_pool, depth, WIDTH_C, page)
    cc_vt = jnp.transpose(cache_c_v, (0, 1, 3, 4, 2)).reshape(n_pool, depth, WIDTH_C, page)
    cc_lf = jnp.pad(jnp.swapaxes(cache_c_logf, 2, 3), ((0, 0), (0, 0), (0, SUBLANES - N_HEADS), (0, 0)))

    cos_p, sin_p = _rope_tables(jnp.arange(t))
    cos_s, sin_s = _rope_tables(jnp.tile(past + jnp.arange(s_new), bs))

    mod = _modulation(jnp.concatenate([c_prompt, c_sample], axis=0), w_mod, b_mod)

    feat_major = lambda w: jnp.zeros((bp, depth, w, t), F32)
    caches = [feat_major(WIDTH_A), jnp.zeros((bp, depth, t * N_HEADS, 2 * HEAD_DIM), F32),
              feat_major(WIDTH_B), feat_major(WIDTH_B), feat_major(WIDTH_C), feat_major(WIDTH_C), feat_major(N_HEADS)]

    xp = x_prompt.reshape(bp * t, d)
    xs = x_sample.reshape(bs * s_new, d)
    rows_s = []
    for l in range(depth):
        lam_init = 0.8 - 0.6 * math.exp(-0.3 * l)
        g = norm_g[l]
        g0, g1, g2, g3 = g[0:1], g[1:2], g[2:3], g[3:4]
        dl, sg = diff_lambda[l], diff_subln_g[l].reshape(1, 2 * HEAD_DIM)

        sh1, sc1, gt1, sh2, sc2, gt2 = [m.reshape(bp, 1, d) for m in jnp.split(mod[l, :bp], 6, axis=-1)]
        q, qbf, kb16, vt16, cum_rep, cum_t, kmean, caches = _inproj_prompt(
            xp, sc1, sh1, g0, cos_p, sin_p, w_qkv[l], w_f[l], b_f[l], caches, l, seq=t)
        q3 = q.reshape(bp, t, -1)
        k3 = kb16.reshape(bp, t, -1)
        ya = _attn_a(q3, k3, vt16, dl, sg, lam_init)
        yb = _attn_b(q3, qbf.reshape(bp, t, WIDTH_B), k3, vt16, kmean.reshape(bp, t // MOBA_BLOCK, WIDTH_B))
        yc = _attn_c(q3, k3, vt16, cum_rep.reshape(bp, t, -1), cum_t)
        xp = _merge(xp, sc1, sh1, gt1, g0, g1, ya.reshape(bp * t, -1), yb.reshape(bp * t, -1), yc.reshape(bp * t, -1),
                    w_g[l], wa[l], wb[l], wc[l], wo[l], rows_per_seq=t, per_row_mod=False)
        xp = _ffn(xp, sc2, sh2, gt2, g2, g3, w1[l], w2[l], rows_per_seq=t, per_row_mod=False)

        sh1, sc1, gt1, sh2, sc2, gt2 = jnp.split(jnp.repeat(mod[l, bp:], s_new, axis=0), 6, axis=-1)
        q, qbf, kvf, lf = _inproj_sample(xs, sc1, sh1, g0, cos_s, sin_s, w_qkv[l], w_f[l], b_f[l])
        q3 = q.reshape(bs, s_new, -1)
        kvf3 = kvf.reshape(bs, s_new, -1)
        kvp = _pad_rows(kvf3, SUBLANES)
        o = 0
        ka_n = kvp[:, :, o:o + WIDTH_A]; o += WIDTH_A
        va_n = kvp[:, :, o:o + WIDTH_A]; o += WIDTH_A
        kb_n = kvp[:, :, o:o + WIDTH_B]; o += WIDTH_B
        vb_n = kvp[:, :, o:o + WIDTH_B]; o += WIDTH_B
        kc_n = kvp[:, :, o:o + WIDTH_C]; o += WIDTH_C
        vc_n = kvp[:, :, o:o + WIDTH_C]
        qa_bd = _block_diag_queries(q3[:, :, :WIDTH_A], WIDTH_A)
        qb_bd = _block_diag_queries(q3[:, :, WIDTH_A:WIDTH_A + WIDTH_B], WIDTH_B)
        qc_bd = _block_diag_queries(q3[:, :, WIDTH_A + WIDTH_B:], WIDTH_C)
        qbf_bd = _block_diag_queries(qbf.reshape(bs, s_new, WIDTH_B), WIDTH_B)
        lf3 = lf.reshape(bs, s_new, LANES)
        lf_new_t = jnp.pad(jnp.swapaxes(lf3[:, :, :N_HEADS], 1, 2),
                           ((0, 0), (0, SUBLANES - N_HEADS), (0, page - s_new)))
        ya = _dec_a(page_table, qa_bd, ka_n, va_n, ca_kt, ca_v, dl, sg, l, lam_init, s_new, 1)
        yb = _dec_b(page_table, qb_bd, qbf_bd, kb_n, vb_n, cb_kt, cb_vt, l, s_new, dec_bpb)
        yc = _dec_c(page_table, qc_bd, kc_n, vc_n, lf_new_t, cc_kt, cc_vt, cc_lf, l, s_new, dec_bpb)
        n_s = bs * s_new
        xs = _merge(xs, sc1, sh1, gt1, g0, g1, ya.reshape(n_s, -1), yb.reshape(n_s, -1), yc.reshape(n_s, -1),
                    w_g[l], wa[l], wb[l], wc[l], wo[l], rows_per_seq=s_new, per_row_mod=True)
        xs = _ffn(xs, sc2, sh2, gt2, g2, g3, w1[l], w2[l], rows_per_seq=s_new, per_row_mod=True)
        rows_s.append((kvf3, lf3))

    akt, av, bkt, bvt, ckt, cvt, lft = caches
    tok_major = lambda c: jnp.transpose(c.reshape(bp, depth, N_HEADS, HEAD_DIM, t), (0, 1, 4, 2, 3))
    prompt_rows = (jnp.transpose(akt.reshape(bp, depth, 2, N_HEADS, HEAD_DIM, t), (0, 1, 5, 2, 3, 4)),
                   av.reshape(bp, depth, t, N_HEADS, 2 * HEAD_DIM),
                   tok_major(bkt), tok_major(bvt), tok_major(ckt), tok_major(cvt), jnp.swapaxes(lft, 2, 3))

    kv = jnp.stack([r[0] for r in rows_s], axis=1)
    o = 0
    a_k = kv[..., o:o + WIDTH_A].reshape(bs, depth, s_new, 2, N_HEADS, HEAD_DIM); o += WIDTH_A
    a_v = kv[..., o:o + WIDTH_A].reshape(bs, depth, s_new, N_HEADS, 2 * HEAD_DIM); o += WIDTH_A
    b_k = kv[..., o:o + WIDTH_B].reshape(bs, depth, s_new, N_HEADS, HEAD_DIM); o += WIDTH_B
    b_v = kv[..., o:o + WIDTH_B].reshape(bs, depth, s_new, N_HEADS, HEAD_DIM); o += WIDTH_B
    c_k = kv[..., o:o + WIDTH_C].reshape(bs, depth, s_new, N_HEADS, HEAD_DIM); o += WIDTH_C
    c_v = kv[..., o:o + WIDTH_C].reshape(bs, depth, s_new, N_HEADS, HEAD_DIM)
    c_lf = jnp.stack([r[1][:, :, :N_HEADS] for r in rows_s], axis=1)
    return ((xp.reshape(bp, t, d), xs.reshape(bs, s_new, d)) + prompt_rows + (a_k, a_v, b_k, b_v, c_k, c_v, c_lf))
```

```python
import functools
import math

import jax
import jax.numpy as jnp
from jax import lax
from jax.experimental import pallas as pl
from jax.experimental.pallas import tpu as pltpu

HEAD_DIM = 64
N_HEADS = 4
WIDTH_A = N_HEADS * 2 * HEAD_DIM
WIDTH_B = N_HEADS * HEAD_DIM
WIDTH_C = N_HEADS * HEAD_DIM
QKV_WIDTH = 3 * WIDTH_A + 3 * WIDTH_B + 3 * WIDTH_C
MOBA_BLOCK = 256
MOBA_TOPK = 3
ROPE_THETA = 10000.0
NORM_EPS = 1e-6
LANES = 128
SUBLANES = 8
DEC_GROUPS = 8
ROW_TILE = 256
ATTN_BLOCK = 256
FLASH_LOOKAHEAD = 3
VMEM_LIMIT = 56 * 1024 * 1024

F32 = jnp.float32
BF16 = jnp.bfloat16
HIGHEST = lax.Precision.HIGHEST
LOG2E = math.log2(math.e)
NT_DIMS = (((1,), (1,)), ((), ()))


def _params(*sem):
    return pltpu.CompilerParams(dimension_semantics=sem, vmem_limit_bytes=VMEM_LIMIT)


def _rms(x, g):
    return x * lax.rsqrt(jnp.mean(x * x, axis=-1, keepdims=True) + NORM_EPS) * g


def _sigmoid(x):
    return 1.0 / (1.0 + jnp.exp(-x))


def _nt_dot(a, b, precision=None):
    return lax.dot_general(a, b, NT_DIMS, precision=precision, preferred_element_type=F32)


def _mod_kernel(c_ref, w_ref, b_ref, o_ref):
    c = c_ref[...]
    s = c * _sigmoid(c)
    o_ref[...] = jnp.dot(s, w_ref[...], precision=HIGHEST, preferred_element_type=F32) + b_ref[...]


def _modulation(c_all, w_mod, b_mod):
    depth, d, n6 = w_mod.shape
    m = c_all.shape[0]
    tn = d
    assert n6 % tn == 0
    return pl.pallas_call(
        _mod_kernel,
        grid=(depth, n6 // tn),
        in_specs=[
            pl.BlockSpec((m, d), lambda l, j: (0, 0)),
            pl.BlockSpec((None, d, tn), lambda l, j: (l, 0, j)),
            pl.BlockSpec((None, 1, tn), lambda l, j: (l, 0, j)),
        ],
        out_specs=pl.BlockSpec((None, m, tn), lambda l, j: (l, 0, j)),
        out_shape=jax.ShapeDtypeStruct((depth, m, n6), F32),
        compiler_params=_params("arbitrary", "arbitrary"),
    )(c_all, w_mod, b_mod.reshape(depth, 1, n6))


def _rope_chunks(z, cos, sin):
    lane = lax.broadcasted_iota(jnp.int32, (z.shape[0], LANES), 1)
    first_half = (lane % HEAD_DIM) < (HEAD_DIM // 2)
    out = []
    for c in range(z.shape[1] // LANES):
        zc = z[:, c * LANES:(c + 1) * LANES]
        rot = jnp.where(first_half, pltpu.roll(zc, LANES - HEAD_DIM // 2, 1), pltpu.roll(zc, HEAD_DIM // 2, 1))
        out.append(zc * cos + rot * sin)
    return jnp.concatenate(out, axis=1)


def _log_sigmoid(z):
    return jnp.minimum(z, 0.0) - jnp.log1p(jnp.exp(-jnp.abs(z)))


def _project(x_ref, sc_ref, sh_ref, g_ref, cos_ref, sin_ref, wqkv_ref, wf_ref, bf_ref):
    x = x_ref[...]
    h = _rms(x, g_ref[...]) * (1.0 + sc_ref[...]) + sh_ref[...]
    hb = h.astype(BF16)
    y = jnp.dot(hb, wqkv_ref[...], preferred_element_type=F32)
    cos = cos_ref[...]
    sin = sin_ref[...]
    scale = HEAD_DIM ** -0.5 * LOG2E
    o = 0
    qa = _rope_chunks(y[:, o:o + WIDTH_A], cos, sin) * scale; o += WIDTH_A
    ka = _rope_chunks(y[:, o:o + WIDTH_A], cos, sin); o += WIDTH_A
    va = y[:, o:o + WIDTH_A]; o += WIDTH_A
    qb = _rope_chunks(y[:, o:o + WIDTH_B], cos, sin) * scale; o += WIDTH_B
    kb = _rope_chunks(y[:, o:o + WIDTH_B], cos, sin); o += WIDTH_B
    vb = y[:, o:o + WIDTH_B]; o += WIDTH_B
    qc = y[:, o:o + WIDTH_C] * scale; o += WIDTH_C
    kc = y[:, o:o + WIDTH_C]; o += WIDTH_C
    vc = y[:, o:o + WIDTH_C]
    q = jnp.concatenate([qa, qb, qc], axis=1).astype(BF16)
    lf = _log_sigmoid(jnp.dot(hb, wf_ref[...], preferred_element_type=F32) + bf_ref[...])
    return q, qb, [ka, va, kb, vb, kc, vc], lf


def _inproj_sample_kernel(x_ref, sc_ref, sh_ref, g_ref, cos_ref, sin_ref, wqkv_ref, wf_ref, bf_ref,
                          q_ref, qbf_ref, kvf_ref, lf_ref):
    q, qb, kv, lf = _project(x_ref, sc_ref, sh_ref, g_ref, cos_ref, sin_ref, wqkv_ref, wf_ref, bf_ref)
    q_ref[...] = q
    qbf_ref[...] = qb
    kvf_ref[...] = jnp.concatenate(kv, axis=1)
    lf_ref[...] = lf


def _inproj_prompt_kernel(x_ref, sc_ref, sh_ref, g_ref, cos_ref, sin_ref, wqkv_ref, wf_ref, bf_ref, tri_ref,
                          akt_in, av_in, bkt_in, bvt_in, ckt_in, cvt_in, lft_in,
                          q_ref, qbf_ref, kb_ref, vt_ref, cumrep_ref, cumt_ref, kmean_ref,
                          akt_ref, av_ref, bkt_ref, bvt_ref, ckt_ref, cvt_ref, lft_ref, carry_ref, *, tiles_per_seq):
    i = pl.program_id(0)

    @pl.when(i % tiles_per_seq == 0)
    def _():
        carry_ref[...] = jnp.zeros_like(carry_ref)

    q, qb, kv, lf = _project(x_ref, sc_ref, sh_ref, g_ref, cos_ref, sin_ref, wqkv_ref, wf_ref, bf_ref)
    ka, va, kb, vb, kc, vc = kv
    tm = va.shape[0]
    vw = 2 * HEAD_DIM
    q_ref[...] = q
    qbf_ref[...] = qb
    kb_ref[...] = jnp.concatenate([ka, kb, kc], axis=1).astype(BF16)
    kmean_ref[...] = jnp.mean(kb, axis=0, keepdims=True)
    vbt, vct = vb.T, vc.T
    akt_ref[...] = ka.T
    bkt_ref[...] = kb.T
    bvt_ref[...] = vbt
    ckt_ref[...] = kc.T
    cvt_ref[...] = vct
    for h in range(N_HEADS):
        av_ref[pl.ds(h, tm, stride=N_HEADS), :] = va[:, h * vw:(h + 1) * vw]
    lft_ref[...] = lf.T[:N_HEADS]
    vt_ref[0:WIDTH_A] = va.T.astype(BF16)
    vt_ref[WIDTH_A:WIDTH_A + WIDTH_B] = vbt.astype(BF16)
    vt_ref[WIDTH_A + WIDTH_B:] = vct.astype(BF16)
    cum = jnp.dot(tri_ref[...], lf, precision=HIGHEST, preferred_element_type=F32) + carry_ref[...]
    cum2 = cum * LOG2E
    cumrep_ref[...] = jnp.concatenate([jnp.broadcast_to(cum2[:, h:h + 1], (tm, LANES)) for h in range(N_HEADS)], axis=1)
    cumt_ref[...] = cum2.T[:SUBLANES]
    carry_ref[...] = cum[tm - 1:, :]


def _mod_spec(mod, layer, part, tm, rows_per_seq):
    if mod.ndim == 5:
        tiles_per_seq = rows_per_seq // tm
        return pl.BlockSpec((None, None, None, 1, mod.shape[-1]), lambda i: (layer, i // tiles_per_seq, part, 0, 0))
    return pl.BlockSpec((None, tm, mod.shape[-1] // 6), lambda i: (layer, i, part))


def _norm_spec(norm_g4, layer, part):
    return pl.BlockSpec((None, None, 1, norm_g4.shape[-1]), lambda i: (layer, part, 0, 0))


def _layer_spec(w, layer):
    return pl.BlockSpec((None,) + w.shape[1:], lambda i: (layer, 0, 0))


def _inproj_specs(d, tm, mod, norm_g4, tab_spec, wqkv, wf, bfg, layer, rows_per_seq):
    return [pl.BlockSpec((tm, d), lambda i: (i, 0)),
            _mod_spec(mod, layer, 1, tm, rows_per_seq), _mod_spec(mod, layer, 0, tm, rows_per_seq),
            _norm_spec(norm_g4, layer, 0), tab_spec, tab_spec,
            _layer_spec(wqkv, layer), _layer_spec(wf, layer), _layer_spec(bfg, layer)]


def _inproj_sample(x2, mod, norm_g4, cos, sin, wqkv, wf, bfg, layer):
    n, d = x2.shape
    tm = min(ROW_TILE, n)
    row = lambda w: pl.BlockSpec((tm, w), lambda i: (i, 0))
    kv_w = 2 * WIDTH_A + 2 * WIDTH_B + 2 * WIDTH_C
    q_w = WIDTH_A + WIDTH_B + WIDTH_C
    return pl.pallas_call(
        _inproj_sample_kernel,
        grid=(n // tm,),
        in_specs=_inproj_specs(d, tm, mod, norm_g4, row(LANES), wqkv, wf, bfg, layer, 1),
        out_specs=[row(q_w), row(WIDTH_B), row(kv_w), row(LANES)],
        out_shape=[jax.ShapeDtypeStruct((n, q_w), BF16), jax.ShapeDtypeStruct((n, WIDTH_B), F32),
                   jax.ShapeDtypeStruct((n, kv_w), F32), jax.ShapeDtypeStruct((n, LANES), F32)],
        compiler_params=_params("arbitrary"),
    )(x2, mod, mod, norm_g4, cos, sin, wqkv, wf, bfg)


def _inproj_prompt(x2, mod, norm_g4, cos, sin, wqkv, wf, bfg, caches, cache_shapes, layer, *, seq):
    n, d = x2.shape
    tm = ROW_TILE
    tps = seq // tm
    nb = n // seq
    tri = (lax.broadcasted_iota(jnp.int32, (tm, tm), 0) >= lax.broadcasted_iota(jnp.int32, (tm, tm), 1)).astype(F32)
    row = lambda w: pl.BlockSpec((tm, w), lambda i: (i, 0))
    q_w = WIDTH_A + WIDTH_B + WIDTH_C
    feat_major = lambda w: pl.BlockSpec((None, None, w, tm), lambda i: (i // tps, layer, 0, i % tps))
    cache_specs = [feat_major(WIDTH_A),
                   pl.BlockSpec((None, None, tm * N_HEADS, 2 * HEAD_DIM), lambda i: (i // tps, layer, i % tps, 0)),
                   feat_major(WIDTH_B), feat_major(WIDTH_B), feat_major(WIDTH_C), feat_major(WIDTH_C),
                   feat_major(N_HEADS)]
    tab_spec = pl.BlockSpec((tm, LANES), lambda i: (i % tps, 0))
    in_specs = (_inproj_specs(d, tm, mod, norm_g4, tab_spec, wqkv, wf, bfg, layer, seq)
                + [pl.BlockSpec((tm, tm), lambda i: (0, 0))])
    n_in, n_out = len(in_specs), 7
    if caches is None:
        aliases, extra = {}, []
        body = lambda *refs: _inproj_prompt_kernel(*refs[:n_in], *([None] * len(cache_shapes)), *refs[n_in:],
                                                   tiles_per_seq=tps)
    else:
        body = functools.partial(_inproj_prompt_kernel, tiles_per_seq=tps)
        aliases = {n_in + k: n_out + k for k in range(len(caches))}
        extra = list(caches)
        in_specs = in_specs + [pl.BlockSpec(memory_space=pl.ANY)] * len(caches)
    outs = pl.pallas_call(
        body,
        grid=(n // tm,),
        in_specs=in_specs,
        out_specs=[row(q_w), row(WIDTH_B), row(q_w),
                   pl.BlockSpec((None, None, q_w, tm), lambda i: (i // tps, i % tps, 0, 0)),
                   row(N_HEADS * LANES),
                   pl.BlockSpec((None, None, SUBLANES, tm), lambda i: (i // tps, i % tps, 0, 0)),
                   pl.BlockSpec((None, 1, WIDTH_B), lambda i: (i, 0, 0))] + cache_specs,
        out_shape=[jax.ShapeDtypeStruct((n, q_w), BF16), jax.ShapeDtypeStruct((n, WIDTH_B), F32),
                   jax.ShapeDtypeStruct((n, q_w), BF16), jax.ShapeDtypeStruct((nb, tps, q_w, tm), BF16),
                   jax.ShapeDtypeStruct((n, N_HEADS * LANES), F32),
                   jax.ShapeDtypeStruct((nb, tps, SUBLANES, tm), F32),
                   jax.ShapeDtypeStruct((n // tm, 1, WIDTH_B), F32)]
        + [jax.ShapeDtypeStruct(s, F32) for s in cache_shapes],
        input_output_aliases=aliases,
        scratch_shapes=[pltpu.VMEM((1, LANES), F32)],
        compiler_params=_params("arbitrary"),
    )(x2, mod, mod, norm_g4, cos, sin, wqkv, wf, bfg, tri, *extra)
    outs = list(outs)
    return tuple(outs[:n_out]) + (outs[n_out:],)


def _head_lane_mask(width, head):
    lane = lax.broadcasted_iota(jnp.int32, (1, width), 1)
    return (lane // HEAD_DIM) == head


def _store_head_queries(q_ref, qm_ref, n_states):
    for g in range(n_states):
        col = (g // 2) * LANES
        qg = q_ref[:, col:col + LANES]
        qm_ref[g] = jnp.where(_head_lane_mask(LANES, g % 2), qg, jnp.zeros_like(qg))


def _flash_states(n_states, qi, block, score_fn, value_fn, m_ref, l_ref, acc_ref, query_bias_fn=None):
    key = lax.broadcasted_iota(jnp.int32, (block, block), 0)
    qry = lax.broadcasted_iota(jnp.int32, (block, block), 1)

    def sweep(j, first):
        start = pl.multiple_of(j * block, block)
        pending = [score_fn(st, j, start) for st in range(min(FLASH_LOOKAHEAD, n_states))]
        for st in range(n_states):
            if st + FLASH_LOOKAHEAD < n_states:
                pending.append(score_fn(st + FLASH_LOOKAHEAD, j, start))
            s = pending[st]
            bias = None if query_bias_fn is None else query_bias_fn(st)
            if first:
                s = jnp.where(qry >= key, s, -jnp.inf)
            s_max = jnp.max(s, axis=0, keepdims=True)
            if bias is not None:
                s_max = s_max + bias
            if first:
                m_new = s_max
                p = jnp.exp2(s - (m_new if bias is None else m_new - bias))
                l_ref[st] = jnp.sum(p, axis=0, keepdims=True)
                acc_ref[st] = jnp.dot(value_fn(st, j), p.astype(BF16), preferred_element_type=F32)
            else:
                m_old = m_ref[st]
                m_new = jnp.maximum(m_old, s_max)
                alpha = jnp.exp2(m_old - m_new)
                p = jnp.exp2(s - (m_new if bias is None else m_new - bias))
                l_ref[st] = alpha * l_ref[st] + jnp.sum(p, axis=0, keepdims=True)
                acc_ref[st] = alpha * acc_ref[st] + jnp.dot(value_fn(st, j), p.astype(BF16),
                                                            preferred_element_type=F32)
            m_ref[st] = m_new

    sweep(qi, True)

    def body(j, carry):
        sweep(j, False)
        return carry

    lax.fori_loop(0, qi, body, 0)


def _flash_scratch(n_states, block):
    return [pltpu.VMEM((n_states, block, LANES), BF16),
            pltpu.VMEM((n_states, 1, block), F32),
            pltpu.VMEM((n_states, 1, block), F32),
            pltpu.VMEM((n_states, LANES, block), F32)]


def _key_spec(t, width, col_block):
    return pl.BlockSpec((None, t, width), lambda bi, qi: (bi, 0, col_block))


def _value_spec(nk, width, blk, row_block):
    return pl.BlockSpec((None, nk, width, blk), lambda bi, qi: (bi, 0, row_block, 0))


def _attn_a_kernel(q_ref, k_ref, vt_ref, lam_ref, g_ref, o_ref, qm_ref, m_ref, l_ref, acc_ref, *, lam_init, block):
    qi = pl.program_id(1)
    n_states = 2 * N_HEADS
    vw = 2 * HEAD_DIM
    _store_head_queries(q_ref, qm_ref, n_states)

    def score(g, j, start):
        kcol = (g // 2) * LANES
        return _nt_dot(k_ref[pl.ds(start, block), kcol:kcol + LANES], qm_ref[g])

    def value(g, j):
        h = g % N_HEADS
        return vt_ref[j, h * vw:(h + 1) * vw, :]

    _flash_states(n_states, qi, block, score, value, m_ref, l_ref, acc_ref)
    lv = lam_ref[...]
    lam = (jnp.exp(jnp.sum(lv[0:1] * lv[1:2], axis=-1, keepdims=True))
           - jnp.exp(jnp.sum(lv[2:3] * lv[3:4], axis=-1, keepdims=True)) + lam_init)
    for h in range(N_HEADS):
        y_t = acc_ref[h] / l_ref[h] - lam * (acc_ref[N_HEADS + h] / l_ref[N_HEADS + h])
        y = _rms(y_t.T, g_ref[...]) * (1.0 - lam_init)
        o_ref[:, h * vw:(h + 1) * vw] = y.astype(o_ref.dtype)


def _attn_a(q3, k3, vt4, diff_lambda, subln_g, lam_init):
    b, t, _ = q3.shape
    blk = ATTN_BLOCK
    nk = t // blk
    return pl.pallas_call(
        functools.partial(_attn_a_kernel, lam_init=lam_init, block=blk),
        grid=(b, nk),
        in_specs=[
            pl.BlockSpec((None, blk, WIDTH_A), lambda bi, qi: (bi, qi, 0)),
            _key_spec(t, WIDTH_A, 0),
            _value_spec(nk, WIDTH_A, blk, 0),
            pl.BlockSpec((4, HEAD_DIM), lambda bi, qi: (0, 0)),
            pl.BlockSpec((1, 2 * HEAD_DIM), lambda bi, qi: (0, 0)),
        ],
        out_specs=pl.BlockSpec((None, blk, WIDTH_A), lambda bi, qi: (bi, qi, 0)),
        out_shape=jax.ShapeDtypeStruct((b, t, WIDTH_A), BF16),
        scratch_shapes=_flash_scratch(2 * N_HEADS, blk),
        compiler_params=_params("arbitrary", "arbitrary"),
    )(q3, k3, vt4, diff_lambda, subln_g)


def _topk_block_mask(gate, valid, axis):
    nb = gate.shape[axis]
    blk = lax.broadcasted_iota(jnp.int32, gate.shape, axis)
    g = jnp.where(valid, gate, -jnp.inf)
    rank = jnp.zeros(gate.shape, jnp.int32)
    for m in range(nb):
        gm = g[m:m + 1, :] if axis == 0 else g[:, m:m + 1]
        ahead = (gm > g) | ((gm == g) & (m < blk))
        rank = rank + ahead.astype(jnp.int32)
    return (valid & (rank < MOBA_TOPK)).astype(F32)


def _store_pair_outputs(o_ref, l_ref, acc_ref):
    for pair in range(N_HEADS // 2):
        even = acc_ref[2 * pair] / l_ref[2 * pair]
        odd = acc_ref[2 * pair + 1] / l_ref[2 * pair + 1]
        y_t = jnp.concatenate([even[:HEAD_DIM], odd[HEAD_DIM:]], axis=0)
        o_ref[:, pair * LANES:(pair + 1) * LANES] = y_t.T.astype(o_ref.dtype)


def _attn_b_kernel(q_ref, qf_ref, k_ref, vt_ref, kmean_ref, o_ref, qm_ref, m_ref, l_ref, acc_ref, sel_ref, *, block):
    qi = pl.program_id(1)
    nb = kmean_ref.shape[0]
    kmean = kmean_ref[...]
    qf = qf_ref[...]
    blk_id = lax.broadcasted_iota(jnp.int32, (nb, block), 0)
    valid = blk_id < qi
    _store_head_queries(q_ref, qm_ref, N_HEADS)
    for h in range(N_HEADS):
        kmean_h = jnp.where(_head_lane_mask(WIDTH_B, h), kmean, 0.0)
        gate = _nt_dot(kmean_h, qf, precision=HIGHEST)
        sel_ref[h] = jnp.where(blk_id == qi, 1.0, _topk_block_mask(gate, valid, 0))

    def score(h, j, start):
        kcol = (h // 2) * LANES
        s = _nt_dot(k_ref[pl.ds(start, block), kcol:kcol + LANES], qm_ref[h])
        return jnp.where(sel_ref[h, pl.ds(j, 1), :] > 0.5, s, -jnp.inf)

    def value(h, j):
        vrow = (h // 2) * LANES
        return vt_ref[j, vrow:vrow + LANES, :]

    _flash_states(N_HEADS, qi, block, score, value, m_ref, l_ref, acc_ref)
    _store_pair_outputs(o_ref, l_ref, acc_ref)


def _attn_b(q3, qbf3, k3, vt4, kmean3):
    b, t, _ = q3.shape
    blk = MOBA_BLOCK
    nb = t // blk
    return pl.pallas_call(
        functools.partial(_attn_b_kernel, block=blk),
        grid=(b, nb),
        in_specs=[
            pl.BlockSpec((None, blk, WIDTH_B), lambda bi, qi: (bi, qi, WIDTH_A // WIDTH_B)),
            pl.BlockSpec((None, blk, WIDTH_B), lambda bi, qi: (bi, qi, 0)),
            _key_spec(t, WIDTH_B, WIDTH_A // WIDTH_B),
            _value_spec(nb, WIDTH_B, blk, WIDTH_A // WIDTH_B),
            pl.BlockSpec((None, nb, WIDTH_B), lambda bi, qi: (bi, 0, 0)),
        ],
        out_specs=pl.BlockSpec((None, blk, WIDTH_B), lambda bi, qi: (bi, qi, 0)),
        out_shape=jax.ShapeDtypeStruct((b, t, WIDTH_B), BF16),
        scratch_shapes=_flash_scratch(N_HEADS, blk) + [pltpu.VMEM((N_HEADS, nb, blk), F32)],
        compiler_params=_params("arbitrary", "arbitrary"),
    )(q3, qbf3, k3, vt4, kmean3)


def _attn_c_kernel(q_ref, k_ref, vt_ref, fq_ref, fk_ref, o_ref, qm_ref, m_ref, l_ref, acc_ref, *, block):
    qi = pl.program_id(1)
    _store_head_queries(q_ref, qm_ref, N_HEADS)

    def score(h, j, start):
        kcol = (h // 2) * LANES
        s = _nt_dot(k_ref[pl.ds(start, block), kcol:kcol + LANES], qm_ref[h])
        f_key = fk_ref[pl.ds(start, block), h * LANES:(h + 1) * LANES]
        return s - jnp.concatenate([f_key] * (block // LANES), axis=1)

    def value(h, j):
        vrow = (h // 2) * LANES
        return vt_ref[j, vrow:vrow + LANES, :]

    _flash_states(N_HEADS, qi, block, score, value, m_ref, l_ref, acc_ref,
                  query_bias_fn=lambda h: fq_ref[h:h + 1, :])
    _store_pair_outputs(o_ref, l_ref, acc_ref)


def _attn_c(q3, k3, vt4, cum_rep3, cum_t):
    b, t, _ = q3.shape
    blk = ATTN_BLOCK
    nk = t // blk
    off = (WIDTH_A + WIDTH_B) // WIDTH_C
    return pl.pallas_call(
        functools.partial(_attn_c_kernel, block=blk),
        grid=(b, nk),
        in_specs=[
            pl.BlockSpec((None, blk, WIDTH_C), lambda bi, qi: (bi, qi, off)),
            _key_spec(t, WIDTH_C, off),
            _value_spec(nk, WIDTH_C, blk, off),
            pl.BlockSpec((None, None, SUBLANES, blk), lambda bi, qi: (bi, qi, 0, 0)),
            pl.BlockSpec((None, t, N_HEADS * LANES), lambda bi, qi: (bi, 0, 0)),
        ],
        out_specs=pl.BlockSpec((None, blk, WIDTH_C), lambda bi, qi: (bi, qi, 0)),
        out_shape=jax.ShapeDtypeStruct((b, t, WIDTH_C), BF16),
        scratch_shapes=_flash_scratch(N_HEADS, blk),
        compiler_params=_params("arbitrary", "arbitrary"),
    )(q3, k3, vt4, cum_t, cum_rep3)


def _tail_page(new, page):
    return jnp.concatenate([new, jnp.zeros((page - new.shape[0], new.shape[1]), new.dtype)], axis=0)


def _tail_mask(rows, page, n_new):
    tok = lax.broadcasted_iota(jnp.int32, (rows, page), 0) // DEC_GROUPS
    key = lax.broadcasted_iota(jnp.int32, (rows, page), 1)
    return (key < n_new) & (key <= tok)


def _sum_list(xs):
    return functools.reduce(lambda a, b: a + b, xs)


def _dec_softmax(s_pages):
    m = functools.reduce(jnp.maximum, [jnp.max(s, axis=-1, keepdims=True) for s in s_pages])
    e_pages = [jnp.exp2(s - m) for s in s_pages]
    l = _sum_list([jnp.sum(e, axis=-1, keepdims=True) for e in e_pages])
    return [e.astype(BF16) for e in e_pages], l


def _collapse_groups(z, n_new):
    return jnp.sum(z.reshape(n_new, DEC_GROUPS, z.shape[1]), axis=1)


def _page_groups(rest, n_groups, n_pages):
    return [rest[i * n_pages:(i + 1) * n_pages] for i in range(n_groups)], rest[n_groups * n_pages]


def _dec_a_kernel(pt_ref, q_ref, kn_ref, vn_ref, lam_ref, g_ref, *rest, n_pages, n_new, lam_init, bpb):
    groups, o_ref = _page_groups(rest, 2 * bpb, n_pages)
    page = groups[0][0].shape[1]
    vw = 2 * HEAD_DIM
    lv = lam_ref[...]
    lam = (jnp.exp(jnp.sum(lv[0:1] * lv[1:2], axis=-1, keepdims=True))
           - jnp.exp(jnp.sum(lv[2:3] * lv[3:4], axis=-1, keepdims=True)) + lam_init)
    g = g_ref[...]
    for u in range(bpb):
        kt_refs, v_refs = groups[u], groups[bpb + u]
        q = q_ref[u]
        rows = q.shape[0]
        s_pages = [jnp.dot(q, r[...].astype(BF16), preferred_element_type=F32) for r in kt_refs]
        s_tail = _nt_dot(q, _tail_page(kn_ref[u], page).astype(BF16))
        s_pages.append(jnp.where(_tail_mask(rows, page, n_new), s_tail, -jnp.inf))
        e_pages, l = _dec_softmax(s_pages)
        v_tail = _tail_page(vn_ref[u], page).astype(BF16)
        acc_heads = []
        for h in range(N_HEADS):
            parts = [jnp.dot(e_pages[p], v_refs[p][pl.ds(h, page, stride=N_HEADS), :].astype(BF16),
                             preferred_element_type=F32) for p in range(n_pages)]
            parts.append(jnp.dot(e_pages[n_pages], v_tail[:, h * vw:(h + 1) * vw], preferred_element_type=F32))
            acc_heads.append(_sum_list(parts))
        acc = jnp.concatenate(acc_heads, axis=1)
        grp = lax.broadcasted_iota(jnp.int32, acc.shape, 0) % DEC_GROUPS
        vhead = lax.broadcasted_iota(jnp.int32, acc.shape, 1) // vw
        coef = jnp.where(grp < N_HEADS, 1.0, -lam)
        z = jnp.where((grp % N_HEADS) == vhead, (acc / l) * coef, 0.0)
        y = _collapse_groups(z, n_new)
        y = jnp.concatenate([_rms(y[:, h * vw:(h + 1) * vw], g) for h in range(N_HEADS)], axis=1) * (1.0 - lam_init)
        o_ref[u] = y.astype(o_ref.dtype)


def _page_specs(n_pages, layer, rows, cols, bpb, u):
    return [pl.BlockSpec((None, None, rows, cols), lambda bi, pt, p=p: (pt[bi * bpb + u, p], layer, 0, 0))
            for p in range(n_pages)]


def _batch_spec(bpb, rows, cols, col_block=0):
    return pl.BlockSpec((bpb, rows, cols), lambda bi, pt: (bi, 0, col_block))


def _dec_a(page_table, qbd, kv_new, cache_kt, cache_v, diff_lambda, subln_g, layer, lam_init, n_new, bpb):
    nb, rows, _ = qbd.shape
    n_pages = page_table.shape[1]
    page = cache_kt.shape[3]
    pad = kv_new.shape[1]
    specs = []
    for u in range(bpb):
        specs += _page_specs(n_pages, layer, WIDTH_A, page, bpb, u)
    for u in range(bpb):
        specs += _page_specs(n_pages, layer, page * N_HEADS, 2 * HEAD_DIM, bpb, u)
    grid_spec = pltpu.PrefetchScalarGridSpec(
        num_scalar_prefetch=1,
        grid=(nb // bpb,),
        in_specs=[
            _batch_spec(bpb, rows, WIDTH_A), _batch_spec(bpb, pad, WIDTH_A, 0), _batch_spec(bpb, pad, WIDTH_A, 1),
            pl.BlockSpec((4, HEAD_DIM), lambda bi, pt: (0, 0)),
            pl.BlockSpec((1, 2 * HEAD_DIM), lambda bi, pt: (0, 0)),
        ] + specs,
        out_specs=_batch_spec(bpb, n_new, WIDTH_A),
    )
    return pl.pallas_call(
        functools.partial(_dec_a_kernel, n_pages=n_pages, n_new=n_new, lam_init=lam_init, bpb=bpb),
        grid_spec=grid_spec,
        out_shape=jax.ShapeDtypeStruct((nb, n_new, WIDTH_A), BF16),
        compiler_params=_params("arbitrary"),
    )(page_table, qbd, kv_new, kv_new, diff_lambda, subln_g,
      *([cache_kt] * (n_pages * bpb)), *([cache_v] * (n_pages * bpb)))


def _dec_pv_t(e_pages, vt_refs, v_new, page):
    n_pages = len(vt_refs)
    parts = [_nt_dot(e_pages[p], vt_refs[p][...].astype(BF16)) for p in range(n_pages)]
    parts.append(jnp.dot(e_pages[n_pages], _tail_page(v_new, page).astype(BF16), preferred_element_type=F32))
    return _sum_list(parts)


def _dec_head_outputs(acc, l, n_new):
    grp = lax.broadcasted_iota(jnp.int32, acc.shape, 0) % DEC_GROUPS
    vhead = lax.broadcasted_iota(jnp.int32, acc.shape, 1) // HEAD_DIM
    return _collapse_groups(jnp.where(grp == vhead, acc / l, 0.0), n_new)


def _dec_b_kernel(pt_ref, q_ref, qf_ref, kn_ref, vn_ref, *rest, n_pages, n_new, bpb):
    groups, o_ref = _page_groups(rest, 2 * bpb, n_pages)
    page = groups[0][0].shape[1]
    pages_per_block = MOBA_BLOCK // page
    n_blocks = n_pages // pages_per_block
    lane_id = lax.broadcasted_iota(jnp.int32, (WIDTH_B, n_blocks), 1)
    for u in range(bpb):
        kt_refs, vt_refs = groups[u], groups[bpb + u]
        q = q_ref[u]
        rows = q.shape[0]
        kt_f32 = [r[...] for r in kt_refs]
        kmean_t = jnp.zeros((WIDTH_B, n_blocks), F32)
        for j in range(n_blocks):
            tot = _sum_list(kt_f32[j * pages_per_block:(j + 1) * pages_per_block])
            col = jnp.sum(tot, axis=1, keepdims=True) * (1.0 / MOBA_BLOCK)
            kmean_t = jnp.where(lane_id == j, col, kmean_t)
        gate = jnp.dot(qf_ref[u], kmean_t, precision=HIGHEST, preferred_element_type=F32)
        sel = _topk_block_mask(gate, jnp.full(gate.shape, True), 1)
        s_pages = []
        for p in range(n_pages):
            s = jnp.dot(q, kt_f32[p].astype(BF16), preferred_element_type=F32)
            j = p // pages_per_block
            s_pages.append(jnp.where(sel[:, j:j + 1] > 0.5, s, -jnp.inf))
        s_tail = _nt_dot(q, _tail_page(kn_ref[u], page).astype(BF16))
        s_pages.append(jnp.where(_tail_mask(rows, page, n_new), s_tail, -jnp.inf))
        e_pages, l = _dec_softmax(s_pages)
        acc = _dec_pv_t(e_pages, vt_refs, vn_ref[u], page)
        o_ref[u] = _dec_head_outputs(acc, l, n_new).astype(o_ref.dtype)


def _dec_b(page_table, qbd, qbd_f32, kv_new, cache_kt, cache_vt, layer, n_new, bpb):
    nb, rows, _ = qbd.shape
    n_pages = page_table.shape[1]
    page = cache_kt.shape[3]
    pad = kv_new.shape[1]
    off = 2 * WIDTH_A // WIDTH_B
    specs = []
    for u in range(2 * bpb):
        specs += _page_specs(n_pages, layer, WIDTH_B, page, bpb, u % bpb)
    grid_spec = pltpu.PrefetchScalarGridSpec(
        num_scalar_prefetch=1,
        grid=(nb // bpb,),
        in_specs=[_batch_spec(bpb, rows, WIDTH_B), _batch_spec(bpb, rows, WIDTH_B),
                  _batch_spec(bpb, pad, WIDTH_B, off), _batch_spec(bpb, pad, WIDTH_B, off + 1)] + specs,
        out_specs=_batch_spec(bpb, n_new, WIDTH_B),
    )
    return pl.pallas_call(
        functools.partial(_dec_b_kernel, n_pages=n_pages, n_new=n_new, bpb=bpb),
        grid_spec=grid_spec,
        out_shape=jax.ShapeDtypeStruct((nb, n_new, WIDTH_B), BF16),
        compiler_params=_params("arbitrary"),
    )(page_table, qbd, qbd_f32, kv_new, kv_new,
      *([cache_kt] * (n_pages * bpb)), *([cache_vt] * (n_pages * bpb)))


def _split3(x):
    hi = x.astype(BF16)
    r1 = x - hi.astype(F32)
    mid = r1.astype(BF16)
    lo = (r1 - mid.astype(F32)).astype(BF16)
    return hi, mid, lo


def _dec_c_kernel(pt_ref, q_ref, kn_ref, vn_ref, lfn_ref, tri_ref, prefix_ref, lf_ref, *rest, n_pages, n_new, bpb):
    groups, o_ref = _page_groups(rest, 2 * bpb, n_pages)
    first = pl.program_id(0) * bpb
    page = groups[0][0].shape[1]
    tri = tri_ref[...]
    for u in range(bpb):
        kt_refs, vt_refs = groups[u], groups[bpb + u]
        q = q_ref[u]
        rows = q.shape[0]
        slots = prefix_ref.shape[0] // SUBLANES
        lf_all = jnp.concatenate([lf_ref[pt_ref[first + u, p]] for p in range(n_pages)] + [lfn_ref[u]]
                                 + [jnp.zeros(((slots - n_pages - 1) * SUBLANES, page), F32)], axis=0)
        both = _sum_list([jnp.dot(part, tri, preferred_element_type=F32) for part in _split3(lf_all)])
        within, totals = both[:, :page], both[:, page:]
        earlier = _sum_list([jnp.dot(prefix_ref[...], part, preferred_element_type=F32) for part in _split3(totals)])
        cum_all = within + earlier
        cum_pages = [cum_all[p * SUBLANES:(p + 1) * SUBLANES] for p in range(n_pages + 1)]
        cum_new = jnp.concatenate([cum_pages[-1]] * n_new, axis=0)
        tok = lax.broadcasted_iota(jnp.int32, (rows, page), 0) // DEC_GROUPS
        key = lax.broadcasted_iota(jnp.int32, (rows, page), 1)
        f_query = jnp.sum(jnp.where(key == tok, cum_new, 0.0), axis=-1, keepdims=True)
        s_pages = []
        for p in range(n_pages + 1):
            if p < n_pages:
                s = jnp.dot(q, kt_refs[p][...].astype(BF16), preferred_element_type=F32)
            else:
                s = _nt_dot(q, _tail_page(kn_ref[u], page).astype(BF16))
            f_key = jnp.concatenate([cum_pages[p]] * n_new, axis=0)
            s_pages.append(s + (f_query - f_key) * LOG2E)
        s_pages[-1] = jnp.where(_tail_mask(rows, page, n_new), s_pages[-1], -jnp.inf)
        e_pages, l = _dec_softmax(s_pages)
        acc = _dec_pv_t(e_pages, vt_refs, vn_ref[u], page)
        o_ref[u] = _dec_head_outputs(acc, l, n_new).astype(o_ref.dtype)


def _dec_c(page_table, qbd, kv_new, lf_new_t, cache_kt, cache_vt, cache_lf_t, layer, n_new, bpb):
    nb, rows, _ = qbd.shape
    n_pages = page_table.shape[1]
    page = cache_kt.shape[3]
    pad = kv_new.shape[1]
    off = (2 * WIDTH_A + 2 * WIDTH_B) // WIDTH_C
    upper = lax.broadcasted_iota(jnp.int32, (page, page), 0) <= lax.broadcasted_iota(jnp.int32, (page, page), 1)
    tri = jnp.concatenate([upper, jnp.full((page, page), True)], axis=1).astype(BF16)
    n_rows = 2 * LANES
    assert (n_pages + 1) * SUBLANES <= n_rows
    r = lax.broadcasted_iota(jnp.int32, (n_rows, n_rows), 0)
    r2 = lax.broadcasted_iota(jnp.int32, (n_rows, n_rows), 1)
    prefix = ((r % SUBLANES == r2 % SUBLANES) & (r2 // SUBLANES < r // SUBLANES)).astype(BF16)
    specs = []
    for u in range(2 * bpb):
        specs += _page_specs(n_pages, layer, WIDTH_C, page, bpb, u % bpb)
    grid_spec = pltpu.PrefetchScalarGridSpec(
        num_scalar_prefetch=1,
        grid=(nb // bpb,),
        in_specs=[_batch_spec(bpb, rows, WIDTH_C), _batch_spec(bpb, pad, WIDTH_C, off), _batch_spec(bpb, pad, WIDTH_C, off + 1),
                  _batch_spec(bpb, SUBLANES, page),
                  pl.BlockSpec((page, 2 * page), lambda bi, pt: (0, 0)),
                  pl.BlockSpec((n_rows, n_rows), lambda bi, pt: (0, 0)),
                  pl.BlockSpec((cache_lf_t.shape[0], None, SUBLANES, page), lambda bi, pt: (0, layer, 0, 0))] + specs,
        out_specs=_batch_spec(bpb, n_new, WIDTH_C),
    )
    return pl.pallas_call(
        functools.partial(_dec_c_kernel, n_pages=n_pages, n_new=n_new, bpb=bpb),
        grid_spec=grid_spec,
        out_shape=jax.ShapeDtypeStruct((nb, n_new, WIDTH_C), BF16),
        compiler_params=_params("arbitrary"),
    )(page_table, qbd, kv_new, kv_new, lf_new_t, tri, prefix, cache_lf_t,
      *([cache_kt] * (n_pages * bpb)), *([cache_vt] * (n_pages * bpb)))


def _merge_kernel(x_ref, sc_ref, sh_ref, gate_ref, g0_ref, g1_ref, ya_ref, yb_ref, yc_ref,
                  wg_ref, wa_ref, wb_ref, wc_ref, wo_ref, o_ref):
    x = x_ref[...]
    d = x.shape[1]
    h = _rms(x, g0_ref[...]) * (1.0 + sc_ref[...]) + sh_ref[...]
    gl = jnp.dot(h.astype(BF16), wg_ref[...], preferred_element_type=F32)
    merged = (_sigmoid(gl[:, :d]) * jnp.dot(ya_ref[...], wa_ref[...], preferred_element_type=F32)
              + _sigmoid(gl[:, d:2 * d]) * jnp.dot(yb_ref[...], wb_ref[...], preferred_element_type=F32)
              + _sigmoid(gl[:, 2 * d:]) * jnp.dot(yc_ref[...], wc_ref[...], preferred_element_type=F32))
    o = jnp.dot(merged.astype(BF16), wo_ref[...], preferred_element_type=F32)
    o_ref[...] = x + gate_ref[...] * _rms(o, g1_ref[...])


def _merge(x2, mod, norm_g4, ya, yb, yc, wg, wa, wb, wc, wo, layer, *, rows_per_seq):
    n, d = x2.shape
    tm = min(ROW_TILE, n)
    row = lambda w: pl.BlockSpec((tm, w), lambda i: (i, 0))
    parts = [_mod_spec(mod, layer, k, tm, rows_per_seq) for k in (1, 0, 2)]
    return pl.pallas_call(
        _merge_kernel,
        grid=(n // tm,),
        in_specs=[row(d)] + parts + [_norm_spec(norm_g4, layer, 0), _norm_spec(norm_g4, layer, 1),
                                     row(WIDTH_A), row(WIDTH_B), row(WIDTH_C)]
        + [_layer_spec(w, layer) for w in (wg, wa, wb, wc, wo)],
        out_specs=row(d),
        out_shape=jax.ShapeDtypeStruct((n, d), F32),
        compiler_params=_params("arbitrary"),
    )(x2, mod, mod, mod, norm_g4, norm_g4, ya, yb, yc, wg, wa, wb, wc, wo)


def _ffn_kernel(x_ref, sc_ref, sh_ref, gate_ref, g2_ref, g3_ref, w1_ref, w2_ref, o_ref, *, chunk):
    x = x_ref[...]
    h = (_rms(x, g2_ref[...]) * (1.0 + sc_ref[...]) + sh_ref[...]).astype(BF16)
    d_ff = w1_ref.shape[1]
    acc = None
    for j in range(d_ff // chunk):
        a = jnp.dot(h, w1_ref[:, j * chunk:(j + 1) * chunk], preferred_element_type=F32)
        a = jnp.square(jnp.maximum(a, 0.0)).astype(BF16)
        part = jnp.dot(a, w2_ref[j * chunk:(j + 1) * chunk, :], preferred_element_type=F32)
        acc = part if acc is None else acc + part
    o_ref[...] = x + gate_ref[...] * _rms(acc, g3_ref[...])


def _ffn(x2, mod, norm_g4, w1, w2, layer, *, rows_per_seq):
    n, d = x2.shape
    tm = min(ROW_TILE, n)
    row = pl.BlockSpec((tm, d), lambda i: (i, 0))
    parts = [_mod_spec(mod, layer, k, tm, rows_per_seq) for k in (4, 3, 5)]
    return pl.pallas_call(
        functools.partial(_ffn_kernel, chunk=min(1024, w1.shape[2])),
        grid=(n // tm,),
        in_specs=[row] + parts + [_norm_spec(norm_g4, layer, 2), _norm_spec(norm_g4, layer, 3),
                                  _layer_spec(w1, layer), _layer_spec(w2, layer)],
        out_specs=row,
        out_shape=jax.ShapeDtypeStruct((n, d), F32),
        compiler_params=_params("arbitrary"),
    )(x2, mod, mod, mod, norm_g4, norm_g4, w1, w2)


def _rope_tables(pos):
    half = HEAD_DIM // 2
    inv = jnp.power(ROPE_THETA, -jnp.arange(half, dtype=F32) * 2.0 / HEAD_DIM)
    ang = pos.astype(F32)[:, None] * inv[None, :]
    cos, sin = jnp.cos(ang), jnp.sin(ang)
    reps = LANES // HEAD_DIM
    return jnp.tile(jnp.concatenate([cos, cos], axis=1), (1, reps)), jnp.tile(jnp.concatenate([-sin, sin], axis=1), (1, reps))


def _block_diag_queries(q, width):
    b, s, _ = q.shape
    grp = jnp.arange(DEC_GROUPS)[:, None]
    col = jnp.arange(width)[None, :] // HEAD_DIM
    keep = (grp == col)[None, None]
    out = jnp.where(keep, q[:, :, None, :], jnp.zeros((), q.dtype))
    return out.reshape(b, s * DEC_GROUPS, width)


def _pad_rows(a, rows):
    return jnp.pad(a, ((0, 0), (0, rows - a.shape[1]), (0, 0)))


def kernel(x_prompt, x_sample, cache_a_k, cache_a_v, cache_b_k, cache_b_v, cache_c_k, cache_c_v, cache_c_logf,
           page_table, c_prompt, c_sample, w_mod, b_mod, norm_g, w_in, b_forget, diff_lambda, diff_subln_g,
           w_branch_a, w_branch_b, w_branch_c, w_out, w_ff1, w_ff2):
    bp, t, d = x_prompt.shape
    bs, s_new, _ = x_sample.shape
    depth = w_mod.shape[0]
    n_pool, _, page = cache_a_k.shape[:3]
    n_pages = page_table.shape[1]
    past = n_pages * page
    assert t % MOBA_BLOCK == 0 and past % MOBA_BLOCK == 0 and MOBA_BLOCK % page == 0
    assert s_new <= SUBLANES and (bs * s_new) % SUBLANES == 0 and ATTN_BLOCK == ROW_TILE == MOBA_BLOCK
    dec_bpb = 2 if bs % 2 == 0 else 1

    w_qkv = w_in[:, :, :QKV_WIDTH].astype(BF16)
    w_f = jnp.pad(w_in[:, :, QKV_WIDTH:QKV_WIDTH + N_HEADS], ((0, 0), (0, 0), (0, LANES - N_HEADS))).astype(BF16)
    w_g = w_in[:, :, QKV_WIDTH + N_HEADS:].astype(BF16)
    b_f = jnp.pad(b_forget, ((0, 0), (0, LANES - N_HEADS))).reshape(depth, 1, LANES)
    wa, wb, wc = w_branch_a.astype(BF16), w_branch_b.astype(BF16), w_branch_c.astype(BF16)
    wo, w1, w2 = w_out.astype(BF16), w_ff1.astype(BF16), w_ff2.astype(BF16)
    norm_g4 = norm_g.reshape(depth, 4, 1, d)
    subln = diff_subln_g.reshape(depth, 1, 2 * HEAD_DIM)

    ca_kt = jnp.transpose(cache_a_k, (0, 1, 3, 4, 5, 2)).reshape(n_pool, depth, WIDTH_A, page)
    ca_v = cache_a_v.reshape(n_pool, depth, page * N_HEADS, 2 * HEAD_DIM)
    cb_kt = jnp.transpose(cache_b_k, (0, 1, 3, 4, 2)).reshape(n_pool, depth, WIDTH_B, page)
    cb_vt = jnp.transpose(cache_b_v, (0, 1, 3, 4, 2)).reshape(n_pool, depth, WIDTH_B, page)
    cc_kt = jnp.transpose(cache_c_k, (0, 1, 3, 4, 2)).reshape(n_pool, depth, WIDTH_C, page)
    cc_vt = jnp.transpose(cache_c_v, (0, 1, 3, 4, 2)).reshape(n_pool, depth, WIDTH_C, page)
    cc_lf = jnp.pad(jnp.swapaxes(cache_c_logf, 2, 3), ((0, 0), (0, 0), (0, SUBLANES - N_HEADS), (0, 0)))

    cos_p, sin_p = _rope_tables(jnp.arange(t))
    cos_s, sin_s = _rope_tables(jnp.tile(past + jnp.arange(s_new), bs))

    n_s = bs * s_new
    mod = _modulation(jnp.concatenate([c_sample, c_prompt], axis=0), w_mod, b_mod)
    mod_s = jnp.repeat(mod[:, :bs], s_new, axis=1)
    mod_p = mod[:, bs:].reshape(depth, bp, 6, 1, d)

    feat_major = lambda w: (bp, depth, w, t)
    cache_shapes = [feat_major(WIDTH_A), (bp, depth, t * N_HEADS, 2 * HEAD_DIM),
                    feat_major(WIDTH_B), feat_major(WIDTH_B), feat_major(WIDTH_C), feat_major(WIDTH_C), feat_major(N_HEADS)]
    caches = None

    xp = x_prompt.reshape(bp * t, d)
    xs = x_sample.reshape(n_s, d)
    rows_s = []
    for l in range(depth):
        lam_init = 0.8 - 0.6 * math.exp(-0.3 * l)
        dl, sg = diff_lambda[l], subln[l]

        q, qbf, kb16, vt16, cum_rep, cum_t, kmean, caches = _inproj_prompt(
            xp, mod_p, norm_g4, cos_p, sin_p, w_qkv, w_f, b_f, caches, cache_shapes, l, seq=t)
        q3 = q.reshape(bp, t, -1)
        k3 = kb16.reshape(bp, t, -1)
        ya = _attn_a(q3, k3, vt16, dl, sg, lam_init)
        yb = _attn_b(q3, qbf.reshape(bp, t, WIDTH_B), k3, vt16, kmean.reshape(bp, t // MOBA_BLOCK, WIDTH_B))
        yc = _attn_c(q3, k3, vt16, cum_rep.reshape(bp, t, -1), cum_t)
        xp = _merge(xp, mod_p, norm_g4, ya.reshape(bp * t, -1), yb.reshape(bp * t, -1), yc.reshape(bp * t, -1),
                    w_g, wa, wb, wc, wo, l, rows_per_seq=t)
        xp = _ffn(xp, mod_p, norm_g4, w1, w2, l, rows_per_seq=t)

        q, qbf, kvf, lf = _inproj_sample(xs, mod_s, norm_g4, cos_s, sin_s, w_qkv, w_f, b_f, l)
        q3 = q.reshape(bs, s_new, -1)
        kvf3 = kvf.reshape(bs, s_new, -1)
        kv_new = _pad_rows(kvf3, SUBLANES)
        qa_bd = _block_diag_queries(q3[:, :, :WIDTH_A], WIDTH_A)
        qb_bd = _block_diag_queries(q3[:, :, WIDTH_A:WIDTH_A + WIDTH_B], WIDTH_B)
        qc_bd = _block_diag_queries(q3[:, :, WIDTH_A + WIDTH_B:], WIDTH_C)
        qbf_bd = _block_diag_queries(qbf.reshape(bs, s_new, WIDTH_B), WIDTH_B)
        lf3 = lf.reshape(bs, s_new, LANES)
        lf_new_t = jnp.pad(jnp.swapaxes(lf3[:, :, :N_HEADS], 1, 2),
                           ((0, 0), (0, SUBLANES - N_HEADS), (0, page - s_new)))
        ya = _dec_a(page_table, qa_bd, kv_new, ca_kt, ca_v, dl, sg, l, lam_init, s_new, 1)
        yb = _dec_b(page_table, qb_bd, qbf_bd, kv_new, cb_kt, cb_vt, l, s_new, dec_bpb)
        yc = _dec_c(page_table, qc_bd, kv_new, lf_new_t, cc_kt, cc_vt, cc_lf, l, s_new, dec_bpb)
        xs = _merge(xs, mod_s, norm_g4, ya.reshape(n_s, -1), yb.reshape(n_s, -1), yc.reshape(n_s, -1),
                    w_g, wa, wb, wc, wo, l, rows_per_seq=s_new)
        xs = _ffn(xs, mod_s, norm_g4, w1, w2, l, rows_per_seq=s_new)
        rows_s.append((kvf3, lf3))

    akt, av, bkt, bvt, ckt, cvt, lft = caches
    tok_major = lambda c: jnp.transpose(c.reshape(bp, depth, N_HEADS, HEAD_DIM, t), (0, 1, 4, 2, 3))
    prompt_rows = (jnp.transpose(akt.reshape(bp, depth, 2, N_HEADS, HEAD_DIM, t), (0, 1, 5, 2, 3, 4)),
                   av.reshape(bp, depth, t, N_HEADS, 2 * HEAD_DIM),
                   tok_major(bkt), tok_major(bvt), tok_major(ckt), tok_major(cvt), jnp.swapaxes(lft, 2, 3))

    kv = jnp.stack([r[0] for r in rows_s], axis=1)
    o = 0
    a_k = kv[..., o:o + WIDTH_A].reshape(bs, depth, s_new, 2, N_HEADS, HEAD_DIM); o += WIDTH_A
    a_v = kv[..., o:o + WIDTH_A].reshape(bs, depth, s_new, N_HEADS, 2 * HEAD_DIM); o += WIDTH_A
    b_k = kv[..., o:o + WIDTH_B].reshape(bs, depth, s_new, N_HEADS, HEAD_DIM); o += WIDTH_B
    b_v = kv[..., o:o + WIDTH_B].reshape(bs, depth, s_new, N_HEADS, HEAD_DIM); o += WIDTH_B
    c_k = kv[..., o:o + WIDTH_C].reshape(bs, depth, s_new, N_HEADS, HEAD_DIM); o += WIDTH_C
    c_v = kv[..., o:o + WIDTH_C].reshape(bs, depth, s_new, N_HEADS, HEAD_DIM)
    c_lf = jnp.stack([r[1][:, :, :N_HEADS] for r in rows_s], axis=1)
    return ((xp.reshape(bp, t, d), xs.reshape(bs, s_new, d)) + prompt_rows + (a_k, a_v, b_k, b_v, c_k, c_v, c_lf))
```

```python
import functools
import math

import jax
import jax.numpy as jnp
from jax import lax
from jax.experimental import pallas as pl
from jax.experimental.pallas import tpu as pltpu

HEAD_DIM = 64
N_HEADS = 4
WIDTH_A = N_HEADS * 2 * HEAD_DIM
WIDTH_B = N_HEADS * HEAD_DIM
WIDTH_C = N_HEADS * HEAD_DIM
QKV_WIDTH = 3 * WIDTH_A + 3 * WIDTH_B + 3 * WIDTH_C
MOBA_BLOCK = 256
MOBA_TOPK = 3
ROPE_THETA = 10000.0
NORM_EPS = 1e-6
LANES = 128
SUBLANES = 8
DEC_GROUPS = 8
ROW_TILE = 256
ATTN_BLOCK = 256
FLASH_LOOKAHEAD = 3
VMEM_LIMIT = 56 * 1024 * 1024

F32 = jnp.float32
BF16 = jnp.bfloat16
HIGHEST = lax.Precision.HIGHEST
LOG2E = math.log2(math.e)
NT_DIMS = (((1,), (1,)), ((), ()))


def _params(*sem):
    return pltpu.CompilerParams(dimension_semantics=sem, vmem_limit_bytes=VMEM_LIMIT)


def _rms(x, g):
    return x * lax.rsqrt(jnp.mean(x * x, axis=-1, keepdims=True) + NORM_EPS) * g


def _sigmoid(x):
    return 1.0 / (1.0 + jnp.exp(-x))


def _nt_dot(a, b, precision=None):
    return lax.dot_general(a, b, NT_DIMS, precision=precision, preferred_element_type=F32)


def _mod_kernel(c_ref, w_ref, b_ref, o_ref):
    c = c_ref[...]
    s = c * _sigmoid(c)
    o_ref[...] = jnp.dot(s, w_ref[...], precision=HIGHEST, preferred_element_type=F32) + b_ref[...]


def _modulation(c_all, w_mod, b_mod):
    depth, d, n6 = w_mod.shape
    m = c_all.shape[0]
    tn = d
    assert n6 % tn == 0
    return pl.pallas_call(
        _mod_kernel,
        grid=(depth, n6 // tn),
        in_specs=[
            pl.BlockSpec((m, d), lambda l, j: (0, 0)),
            pl.BlockSpec((None, d, tn), lambda l, j: (l, 0, j)),
            pl.BlockSpec((None, 1, tn), lambda l, j: (l, 0, j)),
        ],
        out_specs=pl.BlockSpec((None, m, tn), lambda l, j: (l, 0, j)),
        out_shape=jax.ShapeDtypeStruct((depth, m, n6), F32),
        compiler_params=_params("arbitrary", "arbitrary"),
    )(c_all, w_mod, b_mod.reshape(depth, 1, n6))


def _rope_chunks(z, cos, sin):
    lane = lax.broadcasted_iota(jnp.int32, (z.shape[0], LANES), 1)
    first_half = (lane % HEAD_DIM) < (HEAD_DIM // 2)
    out = []
    for c in range(z.shape[1] // LANES):
        zc = z[:, c * LANES:(c + 1) * LANES]
        rot = jnp.where(first_half, pltpu.roll(zc, LANES - HEAD_DIM // 2, 1), pltpu.roll(zc, HEAD_DIM // 2, 1))
        out.append(zc * cos + rot * sin)
    return jnp.concatenate(out, axis=1)


def _log_sigmoid(z):
    return jnp.minimum(z, 0.0) - jnp.log1p(jnp.exp(-jnp.abs(z)))


def _project(x_ref, sc_ref, sh_ref, g_ref, cos_ref, sin_ref, wqkv_ref, wf_ref, bf_ref):
    x = x_ref[...]
    h = _rms(x, g_ref[...]) * (1.0 + sc_ref[...]) + sh_ref[...]
    hb = h.astype(BF16)
    y = jnp.dot(hb, wqkv_ref[...], preferred_element_type=F32)
    cos = cos_ref[...]
    sin = sin_ref[...]
    scale = HEAD_DIM ** -0.5 * LOG2E
    o = 0
    qa = _rope_chunks(y[:, o:o + WIDTH_A], cos, sin) * scale; o += WIDTH_A
    ka = _rope_chunks(y[:, o:o + WIDTH_A], cos, sin); o += WIDTH_A
    va = y[:, o:o + WIDTH_A]; o += WIDTH_A
    qb = _rope_chunks(y[:, o:o + WIDTH_B], cos, sin) * scale; o += WIDTH_B
    kb = _rope_chunks(y[:, o:o + WIDTH_B], cos, sin); o += WIDTH_B
    vb = y[:, o:o + WIDTH_B]; o += WIDTH_B
    qc = y[:, o:o + WIDTH_C] * scale; o += WIDTH_C
    kc = y[:, o:o + WIDTH_C]; o += WIDTH_C
    vc = y[:, o:o + WIDTH_C]
    q = jnp.concatenate([qa, qb, qc], axis=1).astype(BF16)
    lf = _log_sigmoid(jnp.dot(hb, wf_ref[...], preferred_element_type=F32) + bf_ref[...])
    return q, qb, [ka, va, kb, vb, kc, vc], lf


def _inproj_sample_kernel(x_ref, sc_ref, sh_ref, g_ref, cos_ref, sin_ref, wqkv_ref, wf_ref, bf_ref,
                          q_ref, qbf_ref, kvf_ref, lf_ref):
    q, qb, kv, lf = _project(x_ref, sc_ref, sh_ref, g_ref, cos_ref, sin_ref, wqkv_ref, wf_ref, bf_ref)
    q_ref[...] = q
    qbf_ref[...] = qb
    kvf_ref[...] = jnp.concatenate(kv, axis=1)
    lf_ref[...] = lf


def _inproj_prompt_kernel(x_ref, sc_ref, sh_ref, g_ref, cos_ref, sin_ref, wqkv_ref, wf_ref, bf_ref, tri_ref,
                          akt_in, av_in, bkt_in, bvt_in, ckt_in, cvt_in, lft_in,
                          q_ref, qbf_ref, kb_ref, vt_ref, cumrep_ref, cumt_ref, kmean_ref,
                          akt_ref, av_ref, bkt_ref, bvt_ref, ckt_ref, cvt_ref, lft_ref, carry_ref, *, tiles_per_seq):
    i = pl.program_id(0)
    first_layer = akt_in is None
    if first_layer:
        cache_refs = (akt_ref, av_ref, bkt_ref, bvt_ref, ckt_ref, cvt_ref, lft_ref)
        for ref in cache_refs:
            ref[1:] = jnp.zeros((ref.shape[0] - 1,) + ref.shape[1:], F32)
        akt_ref, av_ref, bkt_ref, bvt_ref, ckt_ref, cvt_ref, lft_ref = [ref.at[0] for ref in cache_refs]

    @pl.when(i % tiles_per_seq == 0)
    def _():
        carry_ref[...] = jnp.zeros_like(carry_ref)

    q, qb, kv, lf = _project(x_ref, sc_ref, sh_ref, g_ref, cos_ref, sin_ref, wqkv_ref, wf_ref, bf_ref)
    ka, va, kb, vb, kc, vc = kv
    tm = va.shape[0]
    vw = 2 * HEAD_DIM
    q_ref[...] = q
    qbf_ref[...] = qb
    kb_ref[...] = jnp.concatenate([ka, kb, kc], axis=1).astype(BF16)
    kmean_ref[...] = jnp.mean(kb, axis=0, keepdims=True)
    vbt, vct = vb.T, vc.T
    akt_ref[...] = ka.T
    bkt_ref[...] = kb.T
    bvt_ref[...] = vbt
    ckt_ref[...] = kc.T
    cvt_ref[...] = vct
    for h in range(N_HEADS):
        av_ref[pl.ds(h, tm, stride=N_HEADS), :] = va[:, h * vw:(h + 1) * vw]
    lft_ref[...] = lf.T[:N_HEADS]
    vt_ref[0:WIDTH_A] = va.T.astype(BF16)
    vt_ref[WIDTH_A:WIDTH_A + WIDTH_B] = vbt.astype(BF16)
    vt_ref[WIDTH_A + WIDTH_B:] = vct.astype(BF16)
    cum = jnp.dot(tri_ref[...], lf, precision=HIGHEST, preferred_element_type=F32) + carry_ref[...]
    cum2 = cum * LOG2E
    cumrep_ref[...] = jnp.concatenate([jnp.broadcast_to(cum2[:, h:h + 1], (tm, LANES)) for h in range(N_HEADS)], axis=1)
    cumt_ref[...] = cum2.T[:SUBLANES]
    carry_ref[...] = cum[tm - 1:, :]


def _mod_spec(mod, layer, part, tm, rows_per_seq):
    if mod.ndim == 5:
        tiles_per_seq = rows_per_seq // tm
        return pl.BlockSpec((None, None, None, 1, mod.shape[-1]), lambda i: (layer, i // tiles_per_seq, part, 0, 0))
    return pl.BlockSpec((None, tm, mod.shape[-1] // 6), lambda i: (layer, i, part))


def _norm_spec(norm_g4, layer, part):
    return pl.BlockSpec((None, None, 1, norm_g4.shape[-1]), lambda i: (layer, part, 0, 0))


def _layer_spec(w, layer):
    return pl.BlockSpec((None,) + w.shape[1:], lambda i: (layer, 0, 0))


def _inproj_specs(d, tm, mod, norm_g4, tab_spec, wqkv, wf, bfg, layer, rows_per_seq):
    return [pl.BlockSpec((tm, d), lambda i: (i, 0)),
            _mod_spec(mod, layer, 1, tm, rows_per_seq), _mod_spec(mod, layer, 0, tm, rows_per_seq),
            _norm_spec(norm_g4, layer, 0), tab_spec, tab_spec,
            _layer_spec(wqkv, layer), _layer_spec(wf, layer), _layer_spec(bfg, layer)]


def _inproj_sample(x2, mod, norm_g4, cos, sin, wqkv, wf, bfg, layer):
    n, d = x2.shape
    tm = min(ROW_TILE, n)
    row = lambda w: pl.BlockSpec((tm, w), lambda i: (i, 0))
    kv_w = 2 * WIDTH_A + 2 * WIDTH_B + 2 * WIDTH_C
    q_w = WIDTH_A + WIDTH_B + WIDTH_C
    return pl.pallas_call(
        _inproj_sample_kernel,
        grid=(n // tm,),
        in_specs=_inproj_specs(d, tm, mod, norm_g4, row(LANES), wqkv, wf, bfg, layer, 1),
        out_specs=[row(q_w), row(WIDTH_B), row(kv_w), row(LANES)],
        out_shape=[jax.ShapeDtypeStruct((n, q_w), BF16), jax.ShapeDtypeStruct((n, WIDTH_B), F32),
                   jax.ShapeDtypeStruct((n, kv_w), F32), jax.ShapeDtypeStruct((n, LANES), F32)],
        compiler_params=_params("arbitrary"),
    )(x2, mod, mod, norm_g4, cos, sin, wqkv, wf, bfg)


def _inproj_prompt(x2, mod, norm_g4, cos, sin, wqkv, wf, bfg, caches, cache_shapes, layer, *, seq):
    n, d = x2.shape
    tm = ROW_TILE
    tps = seq // tm
    nb = n // seq
    tri = (lax.broadcasted_iota(jnp.int32, (tm, tm), 0) >= lax.broadcasted_iota(jnp.int32, (tm, tm), 1)).astype(F32)
    row = lambda w: pl.BlockSpec((tm, w), lambda i: (i, 0))
    q_w = WIDTH_A + WIDTH_B + WIDTH_C
    depth = cache_shapes[0][1]
    layers = None if caches is not None else depth
    first = 0 if caches is None else layer
    feat_major = lambda w: pl.BlockSpec((None, layers, w, tm), lambda i: (i // tps, first, 0, i % tps))
    cache_specs = [feat_major(WIDTH_A),
                   pl.BlockSpec((None, layers, tm * N_HEADS, 2 * HEAD_DIM), lambda i: (i // tps, first, i % tps, 0)),
                   feat_major(WIDTH_B), feat_major(WIDTH_B), feat_major(WIDTH_C), feat_major(WIDTH_C),
                   feat_major(N_HEADS)]
    tab_spec = pl.BlockSpec((tm, LANES), lambda i: (i % tps, 0))
    in_specs = (_inproj_specs(d, tm, mod, norm_g4, tab_spec, wqkv, wf, bfg, layer, seq)
                + [pl.BlockSpec((tm, tm), lambda i: (0, 0))])
    n_in, n_out = len(in_specs), 7
    if caches is None:
        aliases, extra = {}, []
        body = lambda *refs: _inproj_prompt_kernel(*refs[:n_in], *([None] * len(cache_shapes)), *refs[n_in:],
                                                   tiles_per_seq=tps)
    else:
        body = functools.partial(_inproj_prompt_kernel, tiles_per_seq=tps)
        aliases = {n_in + k: n_out + k for k in range(len(caches))}
        extra = list(caches)
        in_specs = in_specs + [pl.BlockSpec(memory_space=pl.ANY)] * len(caches)
    outs = pl.pallas_call(
        body,
        grid=(n // tm,),
        in_specs=in_specs,
        out_specs=[row(q_w), row(WIDTH_B), row(q_w),
                   pl.BlockSpec((None, None, q_w, tm), lambda i: (i // tps, i % tps, 0, 0)),
                   row(N_HEADS * LANES),
                   pl.BlockSpec((None, None, SUBLANES, tm), lambda i: (i // tps, i % tps, 0, 0)),
                   pl.BlockSpec((None, 1, WIDTH_B), lambda i: (i, 0, 0))] + cache_specs,
        out_shape=[jax.ShapeDtypeStruct((n, q_w), BF16), jax.ShapeDtypeStruct((n, WIDTH_B), F32),
                   jax.ShapeDtypeStruct((n, q_w), BF16), jax.ShapeDtypeStruct((nb, tps, q_w, tm), BF16),
                   jax.ShapeDtypeStruct((n, N_HEADS * LANES), F32),
                   jax.ShapeDtypeStruct((nb, tps, SUBLANES, tm), F32),
                   jax.ShapeDtypeStruct((n // tm, 1, WIDTH_B), F32)]
        + [jax.ShapeDtypeStruct(s, F32) for s in cache_shapes],
        input_output_aliases=aliases,
        scratch_shapes=[pltpu.VMEM((1, LANES), F32)],
        compiler_params=_params("arbitrary"),
    )(x2, mod, mod, norm_g4, cos, sin, wqkv, wf, bfg, tri, *extra)
    outs = list(outs)
    return tuple(outs[:n_out]) + (outs[n_out:],)


def _head_lane_mask(width, head):
    lane = lax.broadcasted_iota(jnp.int32, (1, width), 1)
    return (lane // HEAD_DIM) == head


def _store_head_queries(q_ref, qm_ref, n_states):
    for g in range(n_states):
        col = (g // 2) * LANES
        qg = q_ref[:, col:col + LANES]
        qm_ref[g] = jnp.where(_head_lane_mask(LANES, g % 2), qg, jnp.zeros_like(qg))


def _flash_states(n_states, qi, block, score_fn, value_fn, m_ref, l_ref, acc_ref, query_bias_fn=None):
    key = lax.broadcasted_iota(jnp.int32, (block, block), 0)
    qry = lax.broadcasted_iota(jnp.int32, (block, block), 1)

    def sweep(j, first):
        start = pl.multiple_of(j * block, block)
        pending = [score_fn(st, j, start) for st in range(min(FLASH_LOOKAHEAD, n_states))]
        for st in range(n_states):
            if st + FLASH_LOOKAHEAD < n_states:
                pending.append(score_fn(st + FLASH_LOOKAHEAD, j, start))
            s = pending[st]
            bias = None if query_bias_fn is None else query_bias_fn(st)
            if first:
                s = jnp.where(qry >= key, s, -jnp.inf)
            s_max = jnp.max(s, axis=0, keepdims=True)
            if bias is not None:
                s_max = s_max + bias
            if first:
                m_new = s_max
                p = jnp.exp2(s - (m_new if bias is None else m_new - bias))
                l_ref[st] = jnp.sum(p, axis=0, keepdims=True)
                acc_ref[st] = jnp.dot(value_fn(st, j), p.astype(BF16), preferred_element_type=F32)
            else:
                m_old = m_ref[st]
                m_new = jnp.maximum(m_old, s_max)
                alpha = jnp.exp2(m_old - m_new)
                p = jnp.exp2(s - (m_new if bias is None else m_new - bias))
                l_ref[st] = alpha * l_ref[st] + jnp.sum(p, axis=0, keepdims=True)
                acc_ref[st] = alpha * acc_ref[st] + jnp.dot(value_fn(st, j), p.astype(BF16),
                                                            preferred_element_type=F32)
            m_ref[st] = m_new

    sweep(qi, True)

    def body(j, carry):
        sweep(j, False)
        return carry

    lax.fori_loop(0, qi, body, 0)


def _flash_scratch(n_states, block, value_width):
    return [pltpu.VMEM((n_states, block, LANES), BF16),
            pltpu.VMEM((n_states, 1, block), F32),
            pltpu.VMEM((n_states, 1, block), F32),
            pltpu.VMEM((n_states, value_width, block), F32)]


def _key_spec(t, width, col_block):
    return pl.BlockSpec((None, t, width), lambda bi, qi: (bi, 0, col_block))


def _value_spec(nk, width, blk, row_block):
    return pl.BlockSpec((None, nk, width, blk), lambda bi, qi: (bi, 0, row_block, 0))


def _attn_a_kernel(q_ref, k_ref, vt_ref, lam_ref, g_ref, o_ref, qm_ref, m_ref, l_ref, acc_ref, *, lam_init, block):
    qi = pl.program_id(1)
    n_states = 2 * N_HEADS
    vw = 2 * HEAD_DIM
    _store_head_queries(q_ref, qm_ref, n_states)

    def score(g, j, start):
        kcol = (g // 2) * LANES
        return _nt_dot(k_ref[pl.ds(start, block), kcol:kcol + LANES], qm_ref[g])

    def value(g, j):
        h = g % N_HEADS
        return vt_ref[j, h * vw:(h + 1) * vw, :]

    _flash_states(n_states, qi, block, score, value, m_ref, l_ref, acc_ref)
    lv = lam_ref[...]
    lam = (jnp.exp(jnp.sum(lv[0:1] * lv[1:2], axis=-1, keepdims=True))
           - jnp.exp(jnp.sum(lv[2:3] * lv[3:4], axis=-1, keepdims=True)) + lam_init)
    for h in range(N_HEADS):
        y_t = acc_ref[h] / l_ref[h] - lam * (acc_ref[N_HEADS + h] / l_ref[N_HEADS + h])
        y = _rms(y_t.T, g_ref[...]) * (1.0 - lam_init)
        o_ref[:, h * vw:(h + 1) * vw] = y.astype(o_ref.dtype)


def _attn_a(q3, k3, vt4, diff_lambda, subln_g, lam_init):
    b, t, _ = q3.shape
    blk = ATTN_BLOCK
    nk = t // blk
    return pl.pallas_call(
        functools.partial(_attn_a_kernel, lam_init=lam_init, block=blk),
        grid=(b, nk),
        in_specs=[
            pl.BlockSpec((None, blk, WIDTH_A), lambda bi, qi: (bi, qi, 0)),
            _key_spec(t, WIDTH_A, 0),
            _value_spec(nk, WIDTH_A, blk, 0),
            pl.BlockSpec((4, HEAD_DIM), lambda bi, qi: (0, 0)),
            pl.BlockSpec((1, 2 * HEAD_DIM), lambda bi, qi: (0, 0)),
        ],
        out_specs=pl.BlockSpec((None, blk, WIDTH_A), lambda bi, qi: (bi, qi, 0)),
        out_shape=jax.ShapeDtypeStruct((b, t, WIDTH_A), BF16),
        scratch_shapes=_flash_scratch(2 * N_HEADS, blk, 2 * HEAD_DIM),
        compiler_params=_params("arbitrary", "arbitrary"),
    )(q3, k3, vt4, diff_lambda, subln_g)


def _topk_block_mask(gate, valid, axis):
    nb = gate.shape[axis]
    blk = lax.broadcasted_iota(jnp.int32, gate.shape, axis)
    g = jnp.where(valid, gate, -jnp.inf)
    rank = jnp.zeros(gate.shape, jnp.int32)
    for m in range(nb):
        gm = g[m:m + 1, :] if axis == 0 else g[:, m:m + 1]
        ahead = (gm > g) | ((gm == g) & (m < blk))
        rank = rank + ahead.astype(jnp.int32)
    return (valid & (rank < MOBA_TOPK)).astype(F32)


def _store_pair_outputs(o_ref, l_ref, acc_ref):
    for pair in range(N_HEADS // 2):
        even = acc_ref[2 * pair] / l_ref[2 * pair]
        odd = acc_ref[2 * pair + 1] / l_ref[2 * pair + 1]
        y_t = jnp.concatenate([even, odd], axis=0)
        o_ref[:, pair * LANES:(pair + 1) * LANES] = y_t.T.astype(o_ref.dtype)


def _attn_b_kernel(q_ref, qf_ref, k_ref, vt_ref, kmean_ref, o_ref, qm_ref, m_ref, l_ref, acc_ref, sel_ref, *, block):
    qi = pl.program_id(1)
    nb = kmean_ref.shape[0]
    kmean = kmean_ref[...]
    blk_id = lax.broadcasted_iota(jnp.int32, (nb, block), 0)
    valid = blk_id < qi
    _store_head_queries(q_ref, qm_ref, N_HEADS)
    kmean_heads = jnp.concatenate([jnp.where(_head_lane_mask(WIDTH_B, h), kmean, 0.0) for h in range(N_HEADS)], axis=0)
    gates = _nt_dot(kmean_heads, qf_ref[...], precision=HIGHEST)
    for h in range(N_HEADS):
        gate = gates[h * nb:(h + 1) * nb]
        sel_ref[h] = jnp.where(blk_id == qi, 1.0, _topk_block_mask(gate, valid, 0))

    def score(h, j, start):
        kcol = (h // 2) * LANES
        s = _nt_dot(k_ref[pl.ds(start, block), kcol:kcol + LANES], qm_ref[h])
        return jnp.where(sel_ref[h, pl.ds(j, 1), :] > 0.5, s, -jnp.inf)

    def value(h, j):
        return vt_ref[j, h * HEAD_DIM:(h + 1) * HEAD_DIM, :]

    _flash_states(N_HEADS, qi, block, score, value, m_ref, l_ref, acc_ref)
    _store_pair_outputs(o_ref, l_ref, acc_ref)


def _attn_b(q3, qbf3, k3, vt4, kmean3):
    b, t, _ = q3.shape
    blk = MOBA_BLOCK
    nb = t // blk
    return pl.pallas_call(
        functools.partial(_attn_b_kernel, block=blk),
        grid=(b, nb),
        in_specs=[
            pl.BlockSpec((None, blk, WIDTH_B), lambda bi, qi: (bi, qi, WIDTH_A // WIDTH_B)),
            pl.BlockSpec((None, blk, WIDTH_B), lambda bi, qi: (bi, qi, 0)),
            _key_spec(t, WIDTH_B, WIDTH_A // WIDTH_B),
            _value_spec(nb, WIDTH_B, blk, WIDTH_A // WIDTH_B),
            pl.BlockSpec((None, nb, WIDTH_B), lambda bi, qi: (bi, 0, 0)),
        ],
        out_specs=pl.BlockSpec((None, blk, WIDTH_B), lambda bi, qi: (bi, qi, 0)),
        out_shape=jax.ShapeDtypeStruct((b, t, WIDTH_B), BF16),
        scratch_shapes=_flash_scratch(N_HEADS, blk, HEAD_DIM) + [pltpu.VMEM((N_HEADS, nb, blk), F32)],
        compiler_params=_params("arbitrary", "arbitrary"),
    )(q3, qbf3, k3, vt4, kmean3)


def _attn_c_kernel(q_ref, k_ref, vt_ref, fq_ref, fk_ref, o_ref, qm_ref, m_ref, l_ref, acc_ref, *, block):
    qi = pl.program_id(1)
    _store_head_queries(q_ref, qm_ref, N_HEADS)

    def score(h, j, start):
        kcol = (h // 2) * LANES
        s = _nt_dot(k_ref[pl.ds(start, block), kcol:kcol + LANES], qm_ref[h])
        f_key = fk_ref[pl.ds(start, block), h * LANES:(h + 1) * LANES]
        return s - jnp.concatenate([f_key] * (block // LANES), axis=1)

    def value(h, j):
        return vt_ref[j, h * HEAD_DIM:(h + 1) * HEAD_DIM, :]

    _flash_states(N_HEADS, qi, block, score, value, m_ref, l_ref, acc_ref,
                  query_bias_fn=lambda h: fq_ref[h:h + 1, :])
    _store_pair_outputs(o_ref, l_ref, acc_ref)


def _attn_c(q3, k3, vt4, cum_rep3, cum_t):
    b, t, _ = q3.shape
    blk = ATTN_BLOCK
    nk = t // blk
    off = (WIDTH_A + WIDTH_B) // WIDTH_C
    return pl.pallas_call(
        functools.partial(_attn_c_kernel, block=blk),
        grid=(b, nk),
        in_specs=[
            pl.BlockSpec((None, blk, WIDTH_C), lambda bi, qi: (bi, qi, off)),
            _key_spec(t, WIDTH_C, off),
            _value_spec(nk, WIDTH_C, blk, off),
            pl.BlockSpec((None, None, SUBLANES, blk), lambda bi, qi: (bi, qi, 0, 0)),
            pl.BlockSpec((None, t, N_HEADS * LANES), lambda bi, qi: (bi, 0, 0)),
        ],
        out_specs=pl.BlockSpec((None, blk, WIDTH_C), lambda bi, qi: (bi, qi, 0)),
        out_shape=jax.ShapeDtypeStruct((b, t, WIDTH_C), BF16),
        scratch_shapes=_flash_scratch(N_HEADS, blk, HEAD_DIM),
        compiler_params=_params("arbitrary", "arbitrary"),
    )(q3, k3, vt4, cum_t, cum_rep3)


def _tail_page(new, page):
    return jnp.concatenate([new, jnp.zeros((page - new.shape[0], new.shape[1]), new.dtype)], axis=0)


def _tail_mask(rows, page, n_new):
    tok = lax.broadcasted_iota(jnp.int32, (rows, page), 0) // DEC_GROUPS
    key = lax.broadcasted_iota(jnp.int32, (rows, page), 1)
    return (key < n_new) & (key <= tok)


def _sum_list(xs):
    return functools.reduce(lambda a, b: a + b, xs)


def _dec_softmax(s_pages):
    m = functools.reduce(jnp.maximum, [jnp.max(s, axis=-1, keepdims=True) for s in s_pages])
    e_pages = [jnp.exp2(s - m) for s in s_pages]
    l = _sum_list([jnp.sum(e, axis=-1, keepdims=True) for e in e_pages])
    return [e.astype(BF16) for e in e_pages], l


def _collapse_groups(z, n_new):
    return jnp.sum(z.reshape(n_new, DEC_GROUPS, z.shape[1]), axis=1)


def _page_groups(rest, n_groups, n_pages):
    return [rest[i * n_pages:(i + 1) * n_pages] for i in range(n_groups)], rest[n_groups * n_pages]


def _dec_a_kernel(pt_ref, q_ref, kn_ref, vn_ref, lam_ref, g_ref, *rest, n_pages, n_new, lam_init, bpb):
    groups, o_ref = _page_groups(rest, 2 * bpb, n_pages)
    page = groups[0][0].shape[1]
    vw = 2 * HEAD_DIM
    lv = lam_ref[...]
    lam = (jnp.exp(jnp.sum(lv[0:1] * lv[1:2], axis=-1, keepdims=True))
           - jnp.exp(jnp.sum(lv[2:3] * lv[3:4], axis=-1, keepdims=True)) + lam_init)
    g = g_ref[...]
    for u in range(bpb):
        kt_refs, v_refs = groups[u], groups[bpb + u]
        q = q_ref[u]
        rows = q.shape[0]
        s_pages = [jnp.dot(q, r[...].astype(BF16), preferred_element_type=F32) for r in kt_refs]
        s_tail = _nt_dot(q, _tail_page(kn_ref[u], page).astype(BF16))
        s_pages.append(jnp.where(_tail_mask(rows, page, n_new), s_tail, -jnp.inf))
        e_pages, l = _dec_softmax(s_pages)
        v_tail = _tail_page(vn_ref[u], page).astype(BF16)
        acc_heads = []
        for h in range(N_HEADS):
            parts = [jnp.dot(e_pages[p], v_refs[p][pl.ds(h, page, stride=N_HEADS), :].astype(BF16),
                             preferred_element_type=F32) for p in range(n_pages)]
            parts.append(jnp.dot(e_pages[n_pages], v_tail[:, h * vw:(h + 1) * vw], preferred_element_type=F32))
            acc_heads.append(_sum_list(parts))
        acc = jnp.concatenate(acc_heads, axis=1)
        grp = lax.broadcasted_iota(jnp.int32, acc.shape, 0) % DEC_GROUPS
        vhead = lax.broadcasted_iota(jnp.int32, acc.shape, 1) // vw
        coef = jnp.where(grp < N_HEADS, 1.0, -lam)
        z = jnp.where((grp % N_HEADS) == vhead, (acc / l) * coef, 0.0)
        y = _collapse_groups(z, n_new)
        y = jnp.concatenate([_rms(y[:, h * vw:(h + 1) * vw], g) for h in range(N_HEADS)], axis=1) * (1.0 - lam_init)
        o_ref[u] = y.astype(o_ref.dtype)


def _page_specs(n_pages, layer, rows, cols, bpb, u):
    return [pl.BlockSpec((None, None, rows, cols), lambda bi, pt, p=p: (pt[bi * bpb + u, p], layer, 0, 0))
            for p in range(n_pages)]


def _batch_spec(bpb, rows, cols, col_block=0):
    return pl.BlockSpec((bpb, rows, cols), lambda bi, pt: (bi, 0, col_block))


def _dec_a(page_table, qbd, kv_new, cache_kt, cache_v, diff_lambda, subln_g, layer, lam_init, n_new, bpb):
    nb, rows, _ = qbd.shape
    n_pages = page_table.shape[1]
    page = cache_kt.shape[3]
    pad = kv_new.shape[1]
    specs = []
    for u in range(bpb):
        specs += _page_specs(n_pages, layer, WIDTH_A, page, bpb, u)
    for u in range(bpb):
        specs += _page_specs(n_pages, layer, page * N_HEADS, 2 * HEAD_DIM, bpb, u)
    grid_spec = pltpu.PrefetchScalarGridSpec(
        num_scalar_prefetch=1,
        grid=(nb // bpb,),
        in_specs=[
            _batch_spec(bpb, rows, WIDTH_A), _batch_spec(bpb, pad, WIDTH_A, 0), _batch_spec(bpb, pad, WIDTH_A, 1),
            pl.BlockSpec((4, HEAD_DIM), lambda bi, pt: (0, 0)),
            pl.BlockSpec((1, 2 * HEAD_DIM), lambda bi, pt: (0, 0)),
        ] + specs,
        out_specs=_batch_spec(bpb, n_new, WIDTH_A),
    )
    return pl.pallas_call(
        functools.partial(_dec_a_kernel, n_pages=n_pages, n_new=n_new, lam_init=lam_init, bpb=bpb),
        grid_spec=grid_spec,
        out_shape=jax.ShapeDtypeStruct((nb, n_new, WIDTH_A), BF16),
        compiler_params=_params("arbitrary"),
    )(page_table, qbd, kv_new, kv_new, diff_lambda, subln_g,
      *([cache_kt] * (n_pages * bpb)), *([cache_v] * (n_pages * bpb)))


class _PageFetch:
    def __init__(self, pt_ref, hbm_refs, bufs, sem, layer, n_pages, bpb):
        self.pt_ref, self.hbm_refs, self.bufs, self.sem = pt_ref, hbm_refs, bufs, sem
        self.layer, self.n_pages, self.bpb = layer, n_pages, bpb
        self.step = pl.program_id(0)
        self.last = pl.num_programs(0) - 1
        self.slot = self.step % 2
        self.next_step = jnp.minimum(self.step + 1, self.last)

    def _copy(self, arr, step, slot, u, p):
        page_id = self.pt_ref[step * self.bpb + u, p]
        return pltpu.make_async_copy(self.hbm_refs[arr].at[page_id, self.layer],
                                     self.bufs[arr].at[slot, u * self.n_pages + p], self.sem.at[arr, slot])

    def _all(self, step, slot):
        return [self._copy(arr, step, slot, u, p) for arr in range(len(self.bufs))
                for u in range(self.bpb) for p in range(self.n_pages)]

    def begin(self):
        @pl.when(self.step == 0)
        def _():
            for c in self._all(0, 0):
                c.start()

        for c in self._all(self.step, self.slot):
            c.wait()

    def start_next(self, u, p):
        for arr in range(len(self.bufs)):
            self._copy(arr, self.next_step, 1 - self.slot, u, p).start()

    def end(self):
        @pl.when(self.step == self.last)
        def _():
            for c in self._all(self.last, 1 - self.slot):
                c.wait()

    def page(self, arr, u, p):
        return self.bufs[arr][self.slot, u * self.n_pages + p]


def _page_fetch_scratch(bpb, n_pages, shapes):
    return ([pltpu.VMEM((2, bpb * n_pages) + s, F32) for s in shapes]
            + [pltpu.SemaphoreType.DMA((len(shapes), 2))])


def _dec_pv_t(e_pages, vt_pages, v_new, page):
    n_pages = len(vt_pages)
    parts = [_nt_dot(e_pages[p], vt_pages[p].astype(BF16)) for p in range(n_pages)]
    parts.append(jnp.dot(e_pages[n_pages], _tail_page(v_new, page).astype(BF16), preferred_element_type=F32))
    return _sum_list(parts)


def _dec_head_outputs(acc, l, n_new):
    grp = lax.broadcasted_iota(jnp.int32, acc.shape, 0) % DEC_GROUPS
    vhead = lax.broadcasted_iota(jnp.int32, acc.shape, 1) // HEAD_DIM
    return _collapse_groups(jnp.where(grp == vhead, acc / l, 0.0), n_new)


def _dec_b_kernel(pt_ref, q_ref, qf_ref, kn_ref, vn_ref, kt_hbm, vt_hbm, o_ref, kbuf, vbuf, sem,
                  *, n_pages, n_new, bpb, layer):
    fetch = _PageFetch(pt_ref, (kt_hbm, vt_hbm), (kbuf, vbuf), sem, layer, n_pages, bpb)
    fetch.begin()
    page = kbuf.shape[3]
    pages_per_block = MOBA_BLOCK // page
    n_blocks = n_pages // pages_per_block
    lane_id = lax.broadcasted_iota(jnp.int32, (WIDTH_B, n_blocks), 1)
    for u in range(bpb):
        q = q_ref[u]
        rows = q.shape[0]
        kt_f32 = []
        for p in range(n_pages):
            kt_f32.append(fetch.page(0, u, p))
            fetch.start_next(u, p)
        kmean_t = jnp.zeros((WIDTH_B, n_blocks), F32)
        for j in range(n_blocks):
            tot = _sum_list(kt_f32[j * pages_per_block:(j + 1) * pages_per_block])
            col = jnp.sum(tot, axis=1, keepdims=True) * (1.0 / MOBA_BLOCK)
            kmean_t = jnp.where(lane_id == j, col, kmean_t)
        gate = jnp.dot(qf_ref[u], kmean_t, precision=HIGHEST, preferred_element_type=F32)
        sel = _topk_block_mask(gate, jnp.full(gate.shape, True), 1)
        s_pages = []
        for p in range(n_pages):
            s = jnp.dot(q, kt_f32[p].astype(BF16), preferred_element_type=F32)
            j = p // pages_per_block
            s_pages.append(jnp.where(sel[:, j:j + 1] > 0.5, s, -jnp.inf))
        s_tail = _nt_dot(q, _tail_page(kn_ref[u], page).astype(BF16))
        s_pages.append(jnp.where(_tail_mask(rows, page, n_new), s_tail, -jnp.inf))
        e_pages, l = _dec_softmax(s_pages)
        acc = _dec_pv_t(e_pages, [fetch.page(1, u, p) for p in range(n_pages)], vn_ref[u], page)
        o_ref[u] = _dec_head_outputs(acc, l, n_new).astype(o_ref.dtype)
    fetch.end()


def _dec_b(page_table, qbd, qbd_f32, kv_new, cache_kt, cache_vt, layer, n_new, bpb):
    nb, rows, _ = qbd.shape
    n_pages = page_table.shape[1]
    page = cache_kt.shape[3]
    pad = kv_new.shape[1]
    off = 2 * WIDTH_A // WIDTH_B
    hbm = pl.BlockSpec(memory_space=pl.ANY)
    grid_spec = pltpu.PrefetchScalarGridSpec(
        num_scalar_prefetch=1,
        grid=(nb // bpb,),
        in_specs=[_batch_spec(bpb, rows, WIDTH_B), _batch_spec(bpb, rows, WIDTH_B),
                  _batch_spec(bpb, pad, WIDTH_B, off), _batch_spec(bpb, pad, WIDTH_B, off + 1), hbm, hbm],
        out_specs=_batch_spec(bpb, n_new, WIDTH_B),
        scratch_shapes=_page_fetch_scratch(bpb, n_pages, [(WIDTH_B, page), (WIDTH_B, page)]),
    )
    return pl.pallas_call(
        functools.partial(_dec_b_kernel, n_pages=n_pages, n_new=n_new, bpb=bpb, layer=layer),
        grid_spec=grid_spec,
        out_shape=jax.ShapeDtypeStruct((nb, n_new, WIDTH_B), BF16),
        compiler_params=_params("arbitrary"),
    )(page_table, qbd, qbd_f32, kv_new, kv_new, cache_kt, cache_vt)


def _split3(x):
    hi = x.astype(BF16)
    r1 = x - hi.astype(F32)
    mid = r1.astype(BF16)
    lo = (r1 - mid.astype(F32)).astype(BF16)
    return hi, mid, lo


def _dec_c_kernel(pt_ref, q_ref, kn_ref, vn_ref, lfn_ref, tri_ref, prefix_ref, lf_ref, kt_hbm, vt_hbm, o_ref,
                  kbuf, vbuf, sem, *, n_pages, n_new, bpb, layer):
    fetch = _PageFetch(pt_ref, (kt_hbm, vt_hbm), (kbuf, vbuf), sem, layer, n_pages, bpb)
    fetch.begin()
    first = pl.program_id(0) * bpb
    page = kbuf.shape[3]
    tri = tri_ref[...]
    for u in range(bpb):
        q = q_ref[u]
        rows = q.shape[0]
        slots = prefix_ref.shape[0] // SUBLANES
        lf_all = jnp.concatenate([lf_ref[pt_ref[first + u, p]] for p in range(n_pages)] + [lfn_ref[u]]
                                 + [jnp.zeros(((slots - n_pages - 1) * SUBLANES, page), F32)], axis=0)
        both = _sum_list([jnp.dot(part, tri, preferred_element_type=F32) for part in _split3(lf_all)])
        within, totals = both[:, :page], both[:, page:]
        earlier = _sum_list([jnp.dot(prefix_ref[...], part, preferred_element_type=F32) for part in _split3(totals)])
        cum_all = within + earlier
        cum_pages = [cum_all[p * SUBLANES:(p + 1) * SUBLANES] for p in range(n_pages + 1)]
        cum_new = jnp.concatenate([cum_pages[-1]] * n_new, axis=0)
        tok = lax.broadcasted_iota(jnp.int32, (rows, page), 0) // DEC_GROUPS
        key = lax.broadcasted_iota(jnp.int32, (rows, page), 1)
        f_query = jnp.sum(jnp.where(key == tok, cum_new, 0.0), axis=-1, keepdims=True)
        s_pages = []
        for p in range(n_pages + 1):
            if p < n_pages:
                s = jnp.dot(q, fetch.page(0, u, p).astype(BF16), preferred_element_type=F32)
                fetch.start_next(u, p)
            else:
                s = _nt_dot(q, _tail_page(kn_ref[u], page).astype(BF16))
            f_key = jnp.concatenate([cum_pages[p]] * n_new, axis=0)
            s_pages.append(s + (f_query - f_key) * LOG2E)
        s_pages[-1] = jnp.where(_tail_mask(rows, page, n_new), s_pages[-1], -jnp.inf)
        e_pages, l = _dec_softmax(s_pages)
        acc = _dec_pv_t(e_pages, [fetch.page(1, u, p) for p in range(n_pages)], vn_ref[u], page)
        o_ref[u] = _dec_head_outputs(acc, l, n_new).astype(o_ref.dtype)
    fetch.end()


def _dec_c(page_table, qbd, kv_new, lf_new_t, cache_kt, cache_vt, cache_lf_t, layer, n_new, bpb):
    nb, rows, _ = qbd.shape
    n_pages = page_table.shape[1]
    page = cache_kt.shape[3]
    pad = kv_new.shape[1]
    off = (2 * WIDTH_A + 2 * WIDTH_B) // WIDTH_C
    upper = lax.broadcasted_iota(jnp.int32, (page, page), 0) <= lax.broadcasted_iota(jnp.int32, (page, page), 1)
    tri = jnp.concatenate([upper, jnp.full((page, page), True)], axis=1).astype(BF16)
    n_rows = 2 * LANES
    assert (n_pages + 1) * SUBLANES <= n_rows
    r = lax.broadcasted_iota(jnp.int32, (n_rows, n_rows), 0)
    r2 = lax.broadcasted_iota(jnp.int32, (n_rows, n_rows), 1)
    prefix = ((r % SUBLANES == r2 % SUBLANES) & (r2 // SUBLANES < r // SUBLANES)).astype(BF16)
    hbm = pl.BlockSpec(memory_space=pl.ANY)
    grid_spec = pltpu.PrefetchScalarGridSpec(
        num_scalar_prefetch=1,
        grid=(nb // bpb,),
        in_specs=[_batch_spec(bpb, rows, WIDTH_C), _batch_spec(bpb, pad, WIDTH_C, off), _batch_spec(bpb, pad, WIDTH_C, off + 1),
                  _batch_spec(bpb, SUBLANES, page),
                  pl.BlockSpec((page, 2 * page), lambda bi, pt: (0, 0)),
                  pl.BlockSpec((n_rows, n_rows), lambda bi, pt: (0, 0)),
                  pl.BlockSpec((cache_lf_t.shape[0], None, SUBLANES, page), lambda bi, pt: (0, layer, 0, 0)), hbm, hbm],
        out_specs=_batch_spec(bpb, n_new, WIDTH_C),
        scratch_shapes=_page_fetch_scratch(bpb, n_pages, [(WIDTH_C, page), (WIDTH_C, page)]),
    )
    return pl.pallas_call(
        functools.partial(_dec_c_kernel, n_pages=n_pages, n_new=n_new, bpb=bpb, layer=layer),
        grid_spec=grid_spec,
        out_shape=jax.ShapeDtypeStruct((nb, n_new, WIDTH_C), BF16),
        compiler_params=_params("arbitrary"),
    )(page_table, qbd, kv_new, kv_new, lf_new_t, tri, prefix, cache_lf_t, cache_kt, cache_vt)


def _merge_kernel(x_ref, sc_ref, sh_ref, gate_ref, g0_ref, g1_ref, ya_ref, yb_ref, yc_ref,
                  wg_ref, wa_ref, wb_ref, wc_ref, wo_ref, o_ref):
    x = x_ref[...]
    d = x.shape[1]
    h = _rms(x, g0_ref[...]) * (1.0 + sc_ref[...]) + sh_ref[...]
    gl = jnp.dot(h.astype(BF16), wg_ref[...], preferred_element_type=F32)
    merged = (_sigmoid(gl[:, :d]) * jnp.dot(ya_ref[...], wa_ref[...], preferred_element_type=F32)
              + _sigmoid(gl[:, d:2 * d]) * jnp.dot(yb_ref[...], wb_ref[...], preferred_element_type=F32)
              + _sigmoid(gl[:, 2 * d:]) * jnp.dot(yc_ref[...], wc_ref[...], preferred_element_type=F32))
    o = jnp.dot(merged.astype(BF16), wo_ref[...], preferred_element_type=F32)
    o_ref[...] = x + gate_ref[...] * _rms(o, g1_ref[...])


def _merge(x2, mod, norm_g4, ya, yb, yc, wg, wa, wb, wc, wo, layer, *, rows_per_seq):
    n, d = x2.shape
    tm = min(ROW_TILE, n)
    row = lambda w: pl.BlockSpec((tm, w), lambda i: (i, 0))
    parts = [_mod_spec(mod, layer, k, tm, rows_per_seq) for k in (1, 0, 2)]
    return pl.pallas_call(
        _merge_kernel,
        grid=(n // tm,),
        in_specs=[row(d)] + parts + [_norm_spec(norm_g4, layer, 0), _norm_spec(norm_g4, layer, 1),
                                     row(WIDTH_A), row(WIDTH_B), row(WIDTH_C)]
        + [_layer_spec(w, layer) for w in (wg, wa, wb, wc, wo)],
        out_specs=row(d),
        out_shape=jax.ShapeDtypeStruct((n, d), F32),
        compiler_params=_params("arbitrary"),
    )(x2, mod, mod, mod, norm_g4, norm_g4, ya, yb, yc, wg, wa, wb, wc, wo)


def _ffn_kernel(x_ref, sc_ref, sh_ref, gate_ref, g2_ref, g3_ref, w1_ref, w2_ref, o_ref, *, chunk):
    x = x_ref[...]
    h = (_rms(x, g2_ref[...]) * (1.0 + sc_ref[...]) + sh_ref[...]).astype(BF16)
    d_ff = w1_ref.shape[1]
    acc = None
    for j in range(d_ff // chunk):
        a = jnp.dot(h, w1_ref[:, j * chunk:(j + 1) * chunk], preferred_element_type=F32)
        a = jnp.square(jnp.maximum(a, 0.0)).astype(BF16)
        part = jnp.dot(a, w2_ref[j * chunk:(j + 1) * chunk, :], preferred_element_type=F32)
        acc = part if acc is None else acc + part
    o_ref[...] = x + gate_ref[...] * _rms(acc, g3_ref[...])


def _ffn(x2, mod, norm_g4, w1, w2, layer, *, rows_per_seq):
    n, d = x2.shape
    tm = min(ROW_TILE, n)
    row = pl.BlockSpec((tm, d), lambda i: (i, 0))
    parts = [_mod_spec(mod, layer, k, tm, rows_per_seq) for k in (4, 3, 5)]
    return pl.pallas_call(
        functools.partial(_ffn_kernel, chunk=min(1024, w1.shape[2])),
        grid=(n // tm,),
        in_specs=[row] + parts + [_norm_spec(norm_g4, layer, 2), _norm_spec(norm_g4, layer, 3),
                                  _layer_spec(w1, layer), _layer_spec(w2, layer)],
        out_specs=row,
        out_shape=jax.ShapeDtypeStruct((n, d), F32),
        compiler_params=_params("arbitrary"),
    )(x2, mod, mod, mod, norm_g4, norm_g4, w1, w2)


def _rope_tables(pos):
    half = HEAD_DIM // 2
    inv = jnp.power(ROPE_THETA, -jnp.arange(half, dtype=F32) * 2.0 / HEAD_DIM)
    ang = pos.astype(F32)[:, None] * inv[None, :]
    cos, sin = jnp.cos(ang), jnp.sin(ang)
    reps = LANES // HEAD_DIM
    return jnp.tile(jnp.concatenate([cos, cos], axis=1), (1, reps)), jnp.tile(jnp.concatenate([-sin, sin], axis=1), (1, reps))


def _block_diag_queries(q, width):
    b, s, _ = q.shape
    grp = jnp.arange(DEC_GROUPS)[:, None]
    col = jnp.arange(width)[None, :] // HEAD_DIM
    keep = (grp == col)[None, None]
    out = jnp.where(keep, q[:, :, None, :], jnp.zeros((), q.dtype))
    return out.reshape(b, s * DEC_GROUPS, width)


def _pad_rows(a, rows):
    return jnp.pad(a, ((0, 0), (0, rows - a.shape[1]), (0, 0)))


def kernel(x_prompt, x_sample, cache_a_k, cache_a_v, cache_b_k, cache_b_v, cache_c_k, cache_c_v, cache_c_logf,
           page_table, c_prompt, c_sample, w_mod, b_mod, norm_g, w_in, b_forget, diff_lambda, diff_subln_g,
           w_branch_a, w_branch_b, w_branch_c, w_out, w_ff1, w_ff2):
    bp, t, d = x_prompt.shape
    bs, s_new, _ = x_sample.shape
    depth = w_mod.shape[0]
    n_pool, _, page = cache_a_k.shape[:3]
    n_pages = page_table.shape[1]
    past = n_pages * page
    assert t % MOBA_BLOCK == 0 and past % MOBA_BLOCK == 0 and MOBA_BLOCK % page == 0
    assert s_new <= SUBLANES and (bs * s_new) % SUBLANES == 0 and ATTN_BLOCK == ROW_TILE == MOBA_BLOCK
    dec_bpb = 2 if bs % 2 == 0 else 1

    w_qkv = w_in[:, :, :QKV_WIDTH].astype(BF16)
    w_f = jnp.pad(w_in[:, :, QKV_WIDTH:QKV_WIDTH + N_HEADS], ((0, 0), (0, 0), (0, LANES - N_HEADS))).astype(BF16)
    w_g = w_in[:, :, QKV_WIDTH + N_HEADS:].astype(BF16)
    b_f = jnp.pad(b_forget, ((0, 0), (0, LANES - N_HEADS))).reshape(depth, 1, LANES)
    wa, wb, wc = w_branch_a.astype(BF16), w_branch_b.astype(BF16), w_branch_c.astype(BF16)
    wo, w1, w2 = w_out.astype(BF16), w_ff1.astype(BF16), w_ff2.astype(BF16)
    norm_g4 = norm_g.reshape(depth, 4, 1, d)
    subln = diff_subln_g.reshape(depth, 1, 2 * HEAD_DIM)

    ca_kt = jnp.transpose(cache_a_k, (0, 1, 3, 4, 5, 2)).reshape(n_pool, depth, WIDTH_A, page)
    ca_v = cache_a_v.reshape(n_pool, depth, page * N_HEADS, 2 * HEAD_DIM)
    cb_kt = jnp.transpose(cache_b_k, (0, 1, 3, 4, 2)).reshape(n_pool, depth, WIDTH_B, page)
    cb_vt = jnp.transpose(cache_b_v, (0, 1, 3, 4, 2)).reshape(n_pool, depth, WIDTH_B, page)
    cc_kt = jnp.transpose(cache_c_k, (0, 1, 3, 4, 2)).reshape(n_pool, depth, WIDTH_C, page)
    cc_vt = jnp.transpose(cache_c_v, (0, 1, 3, 4, 2)).reshape(n_pool, depth, WIDTH_C, page)
    cc_lf = jnp.pad(jnp.swapaxes(cache_c_logf, 2, 3), ((0, 0), (0, 0), (0, SUBLANES - N_HEADS), (0, 0)))

    cos_p, sin_p = _rope_tables(jnp.arange(t))
    cos_s, sin_s = _rope_tables(jnp.tile(past + jnp.arange(s_new), bs))

    n_s = bs * s_new
    mod = _modulation(jnp.concatenate([c_sample, c_prompt], axis=0), w_mod, b_mod)
    mod_s = jnp.repeat(mod[:, :bs], s_new, axis=1)
    mod_p = mod[:, bs:].reshape(depth, bp, 6, 1, d)

    feat_major = lambda w: (bp, depth, w, t)
    cache_shapes = [feat_major(WIDTH_A), (bp, depth, t * N_HEADS, 2 * HEAD_DIM),
                    feat_major(WIDTH_B), feat_major(WIDTH_B), feat_major(WIDTH_C), feat_major(WIDTH_C), feat_major(N_HEADS)]
    caches = None

    xp = x_prompt.reshape(bp * t, d)
    xs = x_sample.reshape(n_s, d)
    rows_s = []
    for l in range(depth):
        lam_init = 0.8 - 0.6 * math.exp(-0.3 * l)
        dl, sg = diff_lambda[l], subln[l]

        q, qbf, kb16, vt16, cum_rep, cum_t, kmean, caches = _inproj_prompt(
            xp, mod_p, norm_g4, cos_p, sin_p, w_qkv, w_f, b_f, caches, cache_shapes, l, seq=t)
        q3 = q.reshape(bp, t, -1)
        k3 = kb16.reshape(bp, t, -1)
        ya = _attn_a(q3, k3, vt16, dl, sg, lam_init)
        yb = _attn_b(q3, qbf.reshape(bp, t, WIDTH_B), k3, vt16, kmean.reshape(bp, t // MOBA_BLOCK, WIDTH_B))
        yc = _attn_c(q3, k3, vt16, cum_rep.reshape(bp, t, -1), cum_t)
        xp = _merge(xp, mod_p, norm_g4, ya.reshape(bp * t, -1), yb.reshape(bp * t, -1), yc.reshape(bp * t, -1),
                    w_g, wa, wb, wc, wo, l, rows_per_seq=t)
        xp = _ffn(xp, mod_p, norm_g4, w1, w2, l, rows_per_seq=t)

        q, qbf, kvf, lf = _inproj_sample(xs, mod_s, norm_g4, cos_s, sin_s, w_qkv, w_f, b_f, l)
        q3 = q.reshape(bs, s_new, -1)
        kvf3 = kvf.reshape(bs, s_new, -1)
        kv_new = _pad_rows(kvf3, SUBLANES)
        qa_bd = _block_diag_queries(q3[:, :, :WIDTH_A], WIDTH_A)
        qb_bd = _block_diag_queries(q3[:, :, WIDTH_A:WIDTH_A + WIDTH_B], WIDTH_B)
        qc_bd = _block_diag_queries(q3[:, :, WIDTH_A + WIDTH_B:], WIDTH_C)
        qbf_bd = _block_diag_queries(qbf.reshape(bs, s_new, WIDTH_B), WIDTH_B)
        lf3 = lf.reshape(bs, s_new, LANES)
        lf_new_t = jnp.pad(jnp.swapaxes(lf3[:, :, :N_HEADS], 1, 2),
                           ((0, 0), (0, SUBLANES - N_HEADS), (0, page - s_new)))
        ya = _dec_a(page_table, qa_bd, kv_new, ca_kt, ca_v, dl, sg, l, lam_init, s_new, 1)
        yb = _dec_b(page_table, qb_bd, qbf_bd, kv_new, cb_kt, cb_vt, l, s_new, dec_bpb)
        yc = _dec_c(page_table, qc_bd, kv_new, lf_new_t, cc_kt, cc_vt, cc_lf, l, s_new, dec_bpb)
        xs = _merge(xs, mod_s, norm_g4, ya.reshape(n_s, -1), yb.reshape(n_s, -1), yc.reshape(n_s, -1),
                    w_g, wa, wb, wc, wo, l, rows_per_seq=s_new)
        xs = _ffn(xs, mod_s, norm_g4, w1, w2, l, rows_per_seq=s_new)
        rows_s.append((kvf3, lf3))

    akt, av, bkt, bvt, ckt, cvt, lft = caches
    tok_major = lambda c: jnp.transpose(c.reshape(bp, depth, N_HEADS, HEAD_DIM, t), (0, 1, 4, 2, 3))
    prompt_rows = (jnp.transpose(akt.reshape(bp, depth, 2, N_HEADS, HEAD_DIM, t), (0, 1, 5, 2, 3, 4)),
                   av.reshape(bp, depth, t, N_HEADS, 2 * HEAD_DIM),
                   tok_major(bkt), tok_major(bvt), tok_major(ckt), tok_major(cvt), jnp.swapaxes(lft, 2, 3))

    kv = jnp.stack([r[0] for r in rows_s], axis=1)
    o = 0
    a_k = kv[..., o:o + WIDTH_A].reshape(bs, depth, s_new, 2, N_HEADS, HEAD_DIM); o += WIDTH_A
    a_v = kv[..., o:o + WIDTH_A].reshape(bs, depth, s_new, N_HEADS, 2 * HEAD_DIM); o += WIDTH_A
    b_k = kv[..., o:o + WIDTH_B].reshape(bs, depth, s_new, N_HEADS, HEAD_DIM); o += WIDTH_B
    b_v = kv[..., o:o + WIDTH_B].reshape(bs, depth, s_new, N_HEADS, HEAD_DIM); o += WIDTH_B
    c_k = kv[..., o:o + WIDTH_C].reshape(bs, depth, s_new, N_HEADS, HEAD_DIM); o += WIDTH_C
    c_v = kv[..., o:o + WIDTH_C].reshape(bs, depth, s_new, N_HEADS, HEAD_DIM)
    c_lf = jnp.stack([r[1][:, :, :N_HEADS] for r in rows_s], axis=1)
    return ((xp.reshape(bp, t, d), xs.reshape(bs, s_new, d)) + prompt_rows + (a_k, a_v, b_k, b_v, c_k, c_v, c_lf))
```

```python
import functools
import math

import jax
import jax.numpy as jnp
from jax import lax
from jax.experimental import pallas as pl
from jax.experimental.pallas import tpu as pltpu

HEAD_DIM = 64
N_HEADS = 4
WIDTH_A = N_HEADS * 2 * HEAD_DIM
WIDTH_B = N_HEADS * HEAD_DIM
WIDTH_C = N_HEADS * HEAD_DIM
QKV_WIDTH = 3 * WIDTH_A + 3 * WIDTH_B + 3 * WIDTH_C
MOBA_BLOCK = 256
MOBA_TOPK = 3
ROPE_THETA = 10000.0
NORM_EPS = 1e-6
LANES = 128
SUBLANES = 8
DEC_GROUPS = 8
ROW_TILE = 256
ATTN_BLOCK = 256
FLASH_LOOKAHEAD = 3
VMEM_LIMIT = 56 * 1024 * 1024

F32 = jnp.float32
BF16 = jnp.bfloat16
HIGHEST = lax.Precision.HIGHEST
LOG2E = math.log2(math.e)
NT_DIMS = (((1,), (1,)), ((), ()))


def _params(*sem):
    return pltpu.CompilerParams(dimension_semantics=sem, vmem_limit_bytes=VMEM_LIMIT)


def _rms(x, g):
    return x * lax.rsqrt(jnp.mean(x * x, axis=-1, keepdims=True) + NORM_EPS) * g


def _sigmoid(x):
    return 1.0 / (1.0 + jnp.exp(-x))


def _nt_dot(a, b, precision=None):
    return lax.dot_general(a, b, NT_DIMS, precision=precision, preferred_element_type=F32)


def _mod_kernel(c_ref, w_ref, b_ref, o_ref):
    c = c_ref[...]
    s = c * _sigmoid(c)
    o_ref[...] = jnp.dot(s, w_ref[...], precision=HIGHEST, preferred_element_type=F32) + b_ref[...]


def _modulation(c_all, w_mod, b_mod):
    depth, d, n6 = w_mod.shape
    m = c_all.shape[0]
    tn = d
    assert n6 % tn == 0
    return pl.pallas_call(
        _mod_kernel,
        grid=(depth, n6 // tn),
        in_specs=[
            pl.BlockSpec((m, d), lambda l, j: (0, 0)),
            pl.BlockSpec((None, d, tn), lambda l, j: (l, 0, j)),
            pl.BlockSpec((None, 1, tn), lambda l, j: (l, 0, j)),
        ],
        out_specs=pl.BlockSpec((None, m, tn), lambda l, j: (l, 0, j)),
        out_shape=jax.ShapeDtypeStruct((depth, m, n6), F32),
        compiler_params=_params("arbitrary", "arbitrary"),
    )(c_all, w_mod, b_mod.reshape(depth, 1, n6))


def _rope_chunks(z, cos, sin):
    lane = lax.broadcasted_iota(jnp.int32, (z.shape[0], LANES), 1)
    first_half = (lane % HEAD_DIM) < (HEAD_DIM // 2)
    out = []
    for c in range(z.shape[1] // LANES):
        zc = z[:, c * LANES:(c + 1) * LANES]
        rot = jnp.where(first_half, pltpu.roll(zc, LANES - HEAD_DIM // 2, 1), pltpu.roll(zc, HEAD_DIM // 2, 1))
        out.append(zc * cos + rot * sin)
    return jnp.concatenate(out, axis=1)


def _log_sigmoid(z):
    return jnp.minimum(z, 0.0) - jnp.log1p(jnp.exp(-jnp.abs(z)))


def _project(x_ref, sc_ref, sh_ref, g_ref, cos_ref, sin_ref, wqkv_ref, wf_ref, bf_ref):
    x = x_ref[...]
    h = _rms(x, g_ref[...]) * (1.0 + sc_ref[...]) + sh_ref[...]
    hb = h.astype(BF16)
    y = jnp.dot(hb, wqkv_ref[...], preferred_element_type=F32)
    cos = cos_ref[...]
    sin = sin_ref[...]
    scale = HEAD_DIM ** -0.5 * LOG2E
    o = 0
    qa = _rope_chunks(y[:, o:o + WIDTH_A], cos, sin) * scale; o += WIDTH_A
    ka = _rope_chunks(y[:, o:o + WIDTH_A], cos, sin); o += WIDTH_A
    va = y[:, o:o + WIDTH_A]; o += WIDTH_A
    qb = _rope_chunks(y[:, o:o + WIDTH_B], cos, sin) * scale; o += WIDTH_B
    kb = _rope_chunks(y[:, o:o + WIDTH_B], cos, sin); o += WIDTH_B
    vb = y[:, o:o + WIDTH_B]; o += WIDTH_B
    qc = y[:, o:o + WIDTH_C] * scale; o += WIDTH_C
    kc = y[:, o:o + WIDTH_C]; o += WIDTH_C
    vc = y[:, o:o + WIDTH_C]
    q = jnp.concatenate([qa, qb, qc], axis=1).astype(BF16)
    lf = _log_sigmoid(jnp.dot(hb, wf_ref[...], preferred_element_type=F32) + bf_ref[...])
    return q, qb, [ka, va, kb, vb, kc, vc], lf


def _inproj_sample_kernel(x_ref, sc_ref, sh_ref, g_ref, cos_ref, sin_ref, wqkv_ref, wf_ref, bf_ref,
                          q_ref, qbf_ref, kvf_ref, lf_ref):
    q, qb, kv, lf = _project(x_ref, sc_ref, sh_ref, g_ref, cos_ref, sin_ref, wqkv_ref, wf_ref, bf_ref)
    q_ref[...] = q
    qbf_ref[...] = qb
    kvf_ref[...] = jnp.concatenate(kv, axis=1)
    lf_ref[...] = lf


def _inproj_prompt_kernel(x_ref, sc_ref, sh_ref, g_ref, cos_ref, sin_ref, wqkv_ref, wf_ref, bf_ref, tri_ref,
                          akt_in, av_in, bkt_in, bvt_in, ckt_in, cvt_in, lft_in,
                          q_ref, qbf_ref, kb_ref, vt_ref, cumrep_ref, cumt_ref, kmean_ref,
                          akt_ref, av_ref, bkt_ref, bvt_ref, ckt_ref, cvt_ref, lft_ref, carry_ref, *, tiles_per_seq):
    i = pl.program_id(0)
    first_layer = akt_in is None
    if first_layer:
        cache_refs = (akt_ref, av_ref, bkt_ref, bvt_ref, ckt_ref, cvt_ref, lft_ref)
        for ref in cache_refs:
            if ref.shape[0] > 1:
                ref[1:] = jnp.zeros((ref.shape[0] - 1,) + ref.shape[1:], F32)
        akt_ref, av_ref, bkt_ref, bvt_ref, ckt_ref, cvt_ref, lft_ref = [ref.at[0] for ref in cache_refs]

    @pl.when(i % tiles_per_seq == 0)
    def _():
        carry_ref[...] = jnp.zeros_like(carry_ref)

    q, qb, kv, lf = _project(x_ref, sc_ref, sh_ref, g_ref, cos_ref, sin_ref, wqkv_ref, wf_ref, bf_ref)
    ka, va, kb, vb, kc, vc = kv
    tm = va.shape[0]
    vw = 2 * HEAD_DIM
    q_ref[...] = q
    qbf_ref[...] = qb
    kb_ref[...] = jnp.concatenate([ka, kb, kc], axis=1).astype(BF16)
    kmean_ref[...] = jnp.mean(kb, axis=0, keepdims=True)
    vbt, vct = vb.T, vc.T
    akt_ref[...] = ka.T
    bkt_ref[...] = kb.T
    bvt_ref[...] = vbt
    ckt_ref[...] = kc.T
    cvt_ref[...] = vct
    for h in range(N_HEADS):
        av_ref[pl.ds(h, tm, stride=N_HEADS), :] = va[:, h * vw:(h + 1) * vw]
    lft_ref[...] = lf.T[:N_HEADS]
    vt_ref[0:WIDTH_A] = va.T.astype(BF16)
    vt_ref[WIDTH_A:WIDTH_A + WIDTH_B] = vbt.astype(BF16)
    vt_ref[WIDTH_A + WIDTH_B:] = vct.astype(BF16)
    cum = jnp.dot(tri_ref[...], lf, precision=HIGHEST, preferred_element_type=F32) + carry_ref[...]
    cum2 = cum * LOG2E
    cumrep_ref[...] = jnp.concatenate([jnp.broadcast_to(cum2[:, h:h + 1], (tm, LANES)) for h in range(N_HEADS)], axis=1)
    cumt_ref[...] = cum2.T[:SUBLANES]
    carry_ref[...] = cum[tm - 1:, :]


def _mod_spec(mod, layer, part, tm, rows_per_seq):
    if mod.ndim == 5:
        tiles_per_seq = rows_per_seq // tm
        return pl.BlockSpec((None, None, None, 1, mod.shape[-1]), lambda i: (layer, i // tiles_per_seq, part, 0, 0))
    return pl.BlockSpec((None, tm, mod.shape[-1] // 6), lambda i: (layer, i, part))


def _norm_spec(norm_g4, layer, part):
    return pl.BlockSpec((None, None, 1, norm_g4.shape[-1]), lambda i: (layer, part, 0, 0))


def _layer_spec(w, layer):
    return pl.BlockSpec((None,) + w.shape[1:], lambda i: (layer, 0, 0))


def _inproj_specs(d, tm, mod, norm_g4, tab_spec, wqkv, wf, bfg, layer, rows_per_seq):
    return [pl.BlockSpec((tm, d), lambda i: (i, 0)),
            _mod_spec(mod, layer, 1, tm, rows_per_seq), _mod_spec(mod, layer, 0, tm, rows_per_seq),
            _norm_spec(norm_g4, layer, 0), tab_spec, tab_spec,
            _layer_spec(wqkv, layer), _layer_spec(wf, layer), _layer_spec(bfg, layer)]


def _inproj_sample(x2, mod, norm_g4, cos, sin, wqkv, wf, bfg, layer):
    n, d = x2.shape
    tm = min(ROW_TILE, n)
    row = lambda w: pl.BlockSpec((tm, w), lambda i: (i, 0))
    kv_w = 2 * WIDTH_A + 2 * WIDTH_B + 2 * WIDTH_C
    q_w = WIDTH_A + WIDTH_B + WIDTH_C
    return pl.pallas_call(
        _inproj_sample_kernel,
        grid=(n // tm,),
        in_specs=_inproj_specs(d, tm, mod, norm_g4, row(LANES), wqkv, wf, bfg, layer, 1),
        out_specs=[row(q_w), row(WIDTH_B), row(kv_w), row(LANES)],
        out_shape=[jax.ShapeDtypeStruct((n, q_w), BF16), jax.ShapeDtypeStruct((n, WIDTH_B), F32),
                   jax.ShapeDtypeStruct((n, kv_w), F32), jax.ShapeDtypeStruct((n, LANES), F32)],
        compiler_params=_params("arbitrary"),
    )(x2, mod, mod, norm_g4, cos, sin, wqkv, wf, bfg)


def _inproj_prompt(x2, mod, norm_g4, cos, sin, wqkv, wf, bfg, caches, cache_shapes, layer, *, seq):
    n, d = x2.shape
    tm = ROW_TILE
    tps = seq // tm
    nb = n // seq
    tri = (lax.broadcasted_iota(jnp.int32, (tm, tm), 0) >= lax.broadcasted_iota(jnp.int32, (tm, tm), 1)).astype(F32)
    row = lambda w: pl.BlockSpec((tm, w), lambda i: (i, 0))
    q_w = WIDTH_A + WIDTH_B + WIDTH_C
    depth = cache_shapes[0][1]
    layers = None if caches is not None else depth
    first = 0 if caches is None else layer
    feat_major = lambda w: pl.BlockSpec((None, layers, w, tm), lambda i: (i // tps, first, 0, i % tps))
    cache_specs = [feat_major(WIDTH_A),
                   pl.BlockSpec((None, layers, tm * N_HEADS, 2 * HEAD_DIM), lambda i: (i // tps, first, i % tps, 0)),
                   feat_major(WIDTH_B), feat_major(WIDTH_B), feat_major(WIDTH_C), feat_major(WIDTH_C),
                   feat_major(N_HEADS)]
    tab_spec = pl.BlockSpec((tm, LANES), lambda i: (i % tps, 0))
    in_specs = (_inproj_specs(d, tm, mod, norm_g4, tab_spec, wqkv, wf, bfg, layer, seq)
                + [pl.BlockSpec((tm, tm), lambda i: (0, 0))])
    n_in, n_out = len(in_specs), 7
    if caches is None:
        aliases, extra = {}, []
        body = lambda *refs: _inproj_prompt_kernel(*refs[:n_in], *([None] * len(cache_shapes)), *refs[n_in:],
                                                   tiles_per_seq=tps)
    else:
        body = functools.partial(_inproj_prompt_kernel, tiles_per_seq=tps)
        aliases = {n_in + k: n_out + k for k in range(len(caches))}
        extra = list(caches)
        in_specs = in_specs + [pl.BlockSpec(memory_space=pl.ANY)] * len(caches)
    outs = pl.pallas_call(
        body,
        grid=(n // tm,),
        in_specs=in_specs,
        out_specs=[row(q_w), row(WIDTH_B), row(q_w),
                   pl.BlockSpec((None, None, q_w, tm), lambda i: (i // tps, i % tps, 0, 0)),
                   row(N_HEADS * LANES),
                   pl.BlockSpec((None, None, SUBLANES, tm), lambda i: (i // tps, i % tps, 0, 0)),
                   pl.BlockSpec((None, 1, WIDTH_B), lambda i: (i, 0, 0))] + cache_specs,
        out_shape=[jax.ShapeDtypeStruct((n, q_w), BF16), jax.ShapeDtypeStruct((n, WIDTH_B), F32),
                   jax.ShapeDtypeStruct((n, q_w), BF16), jax.ShapeDtypeStruct((nb, tps, q_w, tm), BF16),
                   jax.ShapeDtypeStruct((n, N_HEADS * LANES), F32),
                   jax.ShapeDtypeStruct((nb, tps, SUBLANES, tm), F32),
                   jax.ShapeDtypeStruct((n // tm, 1, WIDTH_B), F32)]
        + [jax.ShapeDtypeStruct(s, F32) for s in cache_shapes],
        input_output_aliases=aliases,
        scratch_shapes=[pltpu.VMEM((1, LANES), F32)],
        compiler_params=_params("arbitrary"),
    )(x2, mod, mod, norm_g4, cos, sin, wqkv, wf, bfg, tri, *extra)
    outs = list(outs)
    return tuple(outs[:n_out]) + (outs[n_out:],)


def _head_lane_mask(width, head):
    lane = lax.broadcasted_iota(jnp.int32, (1, width), 1)
    return (lane // HEAD_DIM) == head


def _store_head_queries(q_ref, qm_ref, n_states):
    for g in range(n_states):
        col = (g // 2) * LANES
        qg = q_ref[:, col:col + LANES]
        qm_ref[g] = jnp.where(_head_lane_mask(LANES, g % 2), qg, jnp.zeros_like(qg))


def _flash_states(n_states, qi, block, score_fn, value_fn, m_ref, l_ref, acc_ref, query_bias_fn=None):
    key = lax.broadcasted_iota(jnp.int32, (block, block), 0)
    qry = lax.broadcasted_iota(jnp.int32, (block, block), 1)

    def sweep(j, first):
        start = pl.multiple_of(j * block, block)
        pending = [score_fn(st, j, start) for st in range(min(FLASH_LOOKAHEAD, n_states))]
        for st in range(n_states):
            if st + FLASH_LOOKAHEAD < n_states:
                pending.append(score_fn(st + FLASH_LOOKAHEAD, j, start))
            s = pending[st]
            bias = None if query_bias_fn is None else query_bias_fn(st)
            if first:
                s = jnp.where(qry >= key, s, -jnp.inf)
            s_max = jnp.max(s, axis=0, keepdims=True)
            if bias is not None:
                s_max = s_max + bias
            if first:
                m_new = s_max
                p = jnp.exp2(s - (m_new if bias is None else m_new - bias))
                l_ref[st] = jnp.sum(p, axis=0, keepdims=True)
                acc_ref[st] = jnp.dot(value_fn(st, j), p.astype(BF16), preferred_element_type=F32)
            else:
                m_old = m_ref[st]
                m_new = jnp.maximum(m_old, s_max)
                alpha = jnp.exp2(m_old - m_new)
                p = jnp.exp2(s - (m_new if bias is None else m_new - bias))
                l_ref[st] = alpha * l_ref[st] + jnp.sum(p, axis=0, keepdims=True)
                acc_ref[st] = alpha * acc_ref[st] + jnp.dot(value_fn(st, j), p.astype(BF16),
                                                            preferred_element_type=F32)
            m_ref[st] = m_new

    sweep(qi, True)

    def body(j, carry):
        sweep(j, False)
        return carry

    lax.fori_loop(0, qi, body, 0)


def _flash_scratch(n_states, block, value_width):
    return [pltpu.VMEM((n_states, block, LANES), BF16),
            pltpu.VMEM((n_states, 1, block), F32),
            pltpu.VMEM((n_states, 1, block), F32),
            pltpu.VMEM((n_states, value_width, block), F32)]


def _key_spec(t, width, col_block):
    return pl.BlockSpec((None, t, width), lambda bi, qi: (bi, 0, col_block))


def _value_spec(nk, width, blk, row_block):
    return pl.BlockSpec((None, nk, width, blk), lambda bi, qi: (bi, 0, row_block, 0))


def _attn_a_kernel(q_ref, k_ref, vt_ref, lam_ref, g_ref, o_ref, qm_ref, m_ref, l_ref, acc_ref, *, lam_init, block):
    qi = pl.program_id(1)
    n_states = 2 * N_HEADS
    vw = 2 * HEAD_DIM
    _store_head_queries(q_ref, qm_ref, n_states)

    def score(g, j, start):
        kcol = (g // 2) * LANES
        return _nt_dot(k_ref[pl.ds(start, block), kcol:kcol + LANES], qm_ref[g])

    def value(g, j):
        h = g % N_HEADS
        return vt_ref[j, h * vw:(h + 1) * vw, :]

    _flash_states(n_states, qi, block, score, value, m_ref, l_ref, acc_ref)
    lv = lam_ref[...]
    lam = (jnp.exp(jnp.sum(lv[0:1] * lv[1:2], axis=-1, keepdims=True))
           - jnp.exp(jnp.sum(lv[2:3] * lv[3:4], axis=-1, keepdims=True)) + lam_init)
    for h in range(N_HEADS):
        y_t = acc_ref[h] / l_ref[h] - lam * (acc_ref[N_HEADS + h] / l_ref[N_HEADS + h])
        y = _rms(y_t.T, g_ref[...]) * (1.0 - lam_init)
        o_ref[:, h * vw:(h + 1) * vw] = y.astype(o_ref.dtype)


def _attn_a(q3, k3, vt4, diff_lambda, subln_g, lam_init):
    b, t, _ = q3.shape
    blk = ATTN_BLOCK
    nk = t // blk
    return pl.pallas_call(
        functools.partial(_attn_a_kernel, lam_init=lam_init, block=blk),
        grid=(b, nk),
        in_specs=[
            pl.BlockSpec((None, blk, WIDTH_A), lambda bi, qi: (bi, qi, 0)),
            _key_spec(t, WIDTH_A, 0),
            _value_spec(nk, WIDTH_A, blk, 0),
            pl.BlockSpec((4, HEAD_DIM), lambda bi, qi: (0, 0)),
            pl.BlockSpec((1, 2 * HEAD_DIM), lambda bi, qi: (0, 0)),
        ],
        out_specs=pl.BlockSpec((None, blk, WIDTH_A), lambda bi, qi: (bi, qi, 0)),
        out_shape=jax.ShapeDtypeStruct((b, t, WIDTH_A), BF16),
        scratch_shapes=_flash_scratch(2 * N_HEADS, blk, 2 * HEAD_DIM),
        compiler_params=_params("arbitrary", "arbitrary"),
    )(q3, k3, vt4, diff_lambda, subln_g)


def _topk_block_mask(gate, valid, axis):
    nb = gate.shape[axis]
    blk = lax.broadcasted_iota(jnp.int32, gate.shape, axis)
    g = jnp.where(valid, gate, -jnp.inf)
    rank = jnp.zeros(gate.shape, jnp.int32)
    for m in range(nb):
        gm = g[m:m + 1, :] if axis == 0 else g[:, m:m + 1]
        ahead = (gm > g) | ((gm == g) & (m < blk))
        rank = rank + ahead.astype(jnp.int32)
    return (valid & (rank < MOBA_TOPK)).astype(F32)


def _store_pair_outputs(o_ref, l_ref, acc_ref):
    for pair in range(N_HEADS // 2):
        even = acc_ref[2 * pair] / l_ref[2 * pair]
        odd = acc_ref[2 * pair + 1] / l_ref[2 * pair + 1]
        y_t = jnp.concatenate([even, odd], axis=0)
        o_ref[:, pair * LANES:(pair + 1) * LANES] = y_t.T.astype(o_ref.dtype)


def _attn_b_kernel(q_ref, qf_ref, k_ref, vt_ref, kmean_ref, o_ref, qm_ref, m_ref, l_ref, acc_ref, sel_ref, *, block):
    qi = pl.program_id(1)
    nb = kmean_ref.shape[0]
    kmean = kmean_ref[...]
    blk_id = lax.broadcasted_iota(jnp.int32, (nb, block), 0)
    valid = blk_id < qi
    _store_head_queries(q_ref, qm_ref, N_HEADS)
    kmean_heads = jnp.concatenate([jnp.where(_head_lane_mask(WIDTH_B, h), kmean, 0.0) for h in range(N_HEADS)], axis=0)
    gates = _nt_dot(kmean_heads, qf_ref[...], precision=HIGHEST)
    for h in range(N_HEADS):
        gate = gates[h * nb:(h + 1) * nb]
        sel_ref[h] = jnp.where(blk_id == qi, 1.0, _topk_block_mask(gate, valid, 0))

    def score(h, j, start):
        kcol = (h // 2) * LANES
        s = _nt_dot(k_ref[pl.ds(start, block), kcol:kcol + LANES], qm_ref[h])
        return jnp.where(sel_ref[h, pl.ds(j, 1), :] > 0.5, s, -jnp.inf)

    def value(h, j):
        return vt_ref[j, h * HEAD_DIM:(h + 1) * HEAD_DIM, :]

    _flash_states(N_HEADS, qi, block, score, value, m_ref, l_ref, acc_ref)
    _store_pair_outputs(o_ref, l_ref, acc_ref)


def _attn_b(q3, qbf3, k3, vt4, kmean3):
    b, t, _ = q3.shape
    blk = MOBA_BLOCK
    nb = t // blk
    return pl.pallas_call(
        functools.partial(_attn_b_kernel, block=blk),
        grid=(b, nb),
        in_specs=[
            pl.BlockSpec((None, blk, WIDTH_B), lambda bi, qi: (bi, qi, WIDTH_A // WIDTH_B)),
            pl.BlockSpec((None, blk, WIDTH_B), lambda bi, qi: (bi, qi, 0)),
            _key_spec(t, WIDTH_B, WIDTH_A // WIDTH_B),
            _value_spec(nb, WIDTH_B, blk, WIDTH_A // WIDTH_B),
            pl.BlockSpec((None, nb, WIDTH_B), lambda bi, qi: (bi, 0, 0)),
        ],
        out_specs=pl.BlockSpec((None, blk, WIDTH_B), lambda bi, qi: (bi, qi, 0)),
        out_shape=jax.ShapeDtypeStruct((b, t, WIDTH_B), BF16),
        scratch_shapes=_flash_scratch(N_HEADS, blk, HEAD_DIM) + [pltpu.VMEM((N_HEADS, nb, blk), F32)],
        compiler_params=_params("arbitrary", "arbitrary"),
    )(q3, qbf3, k3, vt4, kmean3)


def _attn_c_kernel(q_ref, k_ref, vt_ref, fq_ref, fk_ref, o_ref, qm_ref, m_ref, l_ref, acc_ref, *, block):
    qi = pl.program_id(1)
    _store_head_queries(q_ref, qm_ref, N_HEADS)

    def score(h, j, start):
        kcol = (h // 2) * LANES
        s = _nt_dot(k_ref[pl.ds(start, block), kcol:kcol + LANES], qm_ref[h])
        f_key = fk_ref[pl.ds(start, block), h * LANES:(h + 1) * LANES]
        return s - jnp.concatenate([f_key] * (block // LANES), axis=1)

    def value(h, j):
        return vt_ref[j, h * HEAD_DIM:(h + 1) * HEAD_DIM, :]

    _flash_states(N_HEADS, qi, block, score, value, m_ref, l_ref, acc_ref,
                  query_bias_fn=lambda h: fq_ref[h:h + 1, :])
    _store_pair_outputs(o_ref, l_ref, acc_ref)


def _attn_c(q3, k3, vt4, cum_rep3, cum_t):
    b, t, _ = q3.shape
    blk = ATTN_BLOCK
    nk = t // blk
    off = (WIDTH_A + WIDTH_B) // WIDTH_C
    return pl.pallas_call(
        functools.partial(_attn_c_kernel, block=blk),
        grid=(b, nk),
        in_specs=[
            pl.BlockSpec((None, blk, WIDTH_C), lambda bi, qi: (bi, qi, off)),
            _key_spec(t, WIDTH_C, off),
            _value_spec(nk, WIDTH_C, blk, off),
            pl.BlockSpec((None, None, SUBLANES, blk), lambda bi, qi: (bi, qi, 0, 0)),
            pl.BlockSpec((None, t, N_HEADS * LANES), lambda bi, qi: (bi, 0, 0)),
        ],
        out_specs=pl.BlockSpec((None, blk, WIDTH_C), lambda bi, qi: (bi, qi, 0)),
        out_shape=jax.ShapeDtypeStruct((b, t, WIDTH_C), BF16),
        scratch_shapes=_flash_scratch(N_HEADS, blk, HEAD_DIM),
        compiler_params=_params("arbitrary", "arbitrary"),
    )(q3, k3, vt4, cum_t, cum_rep3)


def _tail_page(new, page):
    return jnp.concatenate([new, jnp.zeros((page - new.shape[0], new.shape[1]), new.dtype)], axis=0)


def _tail_mask(rows, page, n_new):
    tok = lax.broadcasted_iota(jnp.int32, (rows, page), 0) // DEC_GROUPS
    key = lax.broadcasted_iota(jnp.int32, (rows, page), 1)
    return (key < n_new) & (key <= tok)


def _sum_list(xs):
    return functools.reduce(lambda a, b: a + b, xs)


def _dec_softmax(s_pages):
    m = functools.reduce(jnp.maximum, [jnp.max(s, axis=-1, keepdims=True) for s in s_pages])
    e_pages = [jnp.exp2(s - m) for s in s_pages]
    l = _sum_list([jnp.sum(e, axis=-1, keepdims=True) for e in e_pages])
    return [e.astype(BF16) for e in e_pages], l


def _collapse_groups(z, n_new):
    return jnp.sum(z.reshape(n_new, DEC_GROUPS, z.shape[1]), axis=1)


def _page_groups(rest, n_groups, n_pages):
    return [rest[i * n_pages:(i + 1) * n_pages] for i in range(n_groups)], rest[n_groups * n_pages]


def _dec_a_kernel(pt_ref, q_ref, kn_ref, vn_ref, lam_ref, g_ref, *rest, n_pages, n_new, lam_init, bpb):
    groups, o_ref = _page_groups(rest, 2 * bpb, n_pages)
    page = groups[0][0].shape[1]
    vw = 2 * HEAD_DIM
    lv = lam_ref[...]
    lam = (jnp.exp(jnp.sum(lv[0:1] * lv[1:2], axis=-1, keepdims=True))
           - jnp.exp(jnp.sum(lv[2:3] * lv[3:4], axis=-1, keepdims=True)) + lam_init)
    g = g_ref[...]
    for u in range(bpb):
        kt_refs, v_refs = groups[u], groups[bpb + u]
        q = q_ref[u]
        rows = q.shape[0]
        s_pages = [jnp.dot(q, r[...].astype(BF16), preferred_element_type=F32) for r in kt_refs]
        s_tail = _nt_dot(q, _tail_page(kn_ref[u], page).astype(BF16))
        s_pages.append(jnp.where(_tail_mask(rows, page, n_new), s_tail, -jnp.inf))
        e_pages, l = _dec_softmax(s_pages)
        v_tail = _tail_page(vn_ref[u], page).astype(BF16)
        acc_heads = []
        for h in range(N_HEADS):
            parts = [jnp.dot(e_pages[p], v_refs[p][pl.ds(h, page, stride=N_HEADS), :].astype(BF16),
                             preferred_element_type=F32) for p in range(n_pages)]
            parts.append(jnp.dot(e_pages[n_pages], v_tail[:, h * vw:(h + 1) * vw], preferred_element_type=F32))
            acc_heads.append(_sum_list(parts))
        acc = jnp.concatenate(acc_heads, axis=1)
        grp = lax.broadcasted_iota(jnp.int32, acc.shape, 0) % DEC_GROUPS
        vhead = lax.broadcasted_iota(jnp.int32, acc.shape, 1) // vw
        coef = jnp.where(grp < N_HEADS, 1.0, -lam)
        z = jnp.where((grp % N_HEADS) == vhead, (acc / l) * coef, 0.0)
        y = _collapse_groups(z, n_new)
        y = jnp.concatenate([_rms(y[:, h * vw:(h + 1) * vw], g) for h in range(N_HEADS)], axis=1) * (1.0 - lam_init)
        o_ref[u] = y.astype(o_ref.dtype)


def _page_specs(n_pages, layer, rows, cols, bpb, u):
    return [pl.BlockSpec((None, None, rows, cols), lambda bi, pt, p=p: (pt[bi * bpb + u, p], layer, 0, 0))
            for p in range(n_pages)]


def _batch_spec(bpb, rows, cols, col_block=0):
    return pl.BlockSpec((bpb, rows, cols), lambda bi, pt: (bi, 0, col_block))


def _dec_a(page_table, qbd, kv_new, cache_kt, cache_v, diff_lambda, subln_g, layer, lam_init, n_new, bpb):
    nb, rows, _ = qbd.shape
    n_pages = page_table.shape[1]
    page = cache_kt.shape[3]
    pad = kv_new.shape[1]
    specs = []
    for u in range(bpb):
        specs += _page_specs(n_pages, layer, WIDTH_A, page, bpb, u)
    for u in range(bpb):
        specs += _page_specs(n_pages, layer, page * N_HEADS, 2 * HEAD_DIM, bpb, u)
    grid_spec = pltpu.PrefetchScalarGridSpec(
        num_scalar_prefetch=1,
        grid=(nb // bpb,),
        in_specs=[
            _batch_spec(bpb, rows, WIDTH_A), _batch_spec(bpb, pad, WIDTH_A, 0), _batch_spec(bpb, pad, WIDTH_A, 1),
            pl.BlockSpec((4, HEAD_DIM), lambda bi, pt: (0, 0)),
            pl.BlockSpec((1, 2 * HEAD_DIM), lambda bi, pt: (0, 0)),
        ] + specs,
        out_specs=_batch_spec(bpb, n_new, WIDTH_A),
    )
    return pl.pallas_call(
        functools.partial(_dec_a_kernel, n_pages=n_pages, n_new=n_new, lam_init=lam_init, bpb=bpb),
        grid_spec=grid_spec,
        out_shape=jax.ShapeDtypeStruct((nb, n_new, WIDTH_A), BF16),
        compiler_params=_params("arbitrary"),
    )(page_table, qbd, kv_new, kv_new, diff_lambda, subln_g,
      *([cache_kt] * (n_pages * bpb)), *([cache_v] * (n_pages * bpb)))


class _PageFetch:
    def __init__(self, pt_ref, hbm_refs, bufs, sem, layer, n_pages, bpb):
        self.pt_ref, self.hbm_refs, self.bufs, self.sem = pt_ref, hbm_refs, bufs, sem
        self.layer, self.n_pages, self.bpb = layer, n_pages, bpb
        self.step = pl.program_id(0)
        self.last = pl.num_programs(0) - 1
        self.slot = self.step % 2
        self.next_step = jnp.minimum(self.step + 1, self.last)

    def _copy(self, arr, step, slot, u, p):
        page_id = self.pt_ref[step * self.bpb + u, p]
        return pltpu.make_async_copy(self.hbm_refs[arr].at[page_id, self.layer],
                                     self.bufs[arr].at[slot, u * self.n_pages + p], self.sem.at[arr, slot])

    def _all(self, step, slot):
        return [self._copy(arr, step, slot, u, p) for arr in range(len(self.bufs))
                for u in range(self.bpb) for p in range(self.n_pages)]

    def begin(self):
        @pl.when(self.step == 0)
        def _():
            for c in self._all(0, 0):
                c.start()

        for c in self._all(self.step, self.slot):
            c.wait()

    def start_next_early(self, u, p):
        if u == 0:
            for arr in range(len(self.bufs)):
                for u_next in range(self.bpb):
                    self._copy(arr, self.next_step, 1 - self.slot, u_next, p).start()

    def end(self):
        @pl.when(self.step == self.last)
        def _():
            for c in self._all(self.last, 1 - self.slot):
                c.wait()

    def page(self, arr, u, p):
        return self.bufs[arr][self.slot, u * self.n_pages + p]


def _page_fetch_scratch(bpb, n_pages, shapes):
    return ([pltpu.VMEM((2, bpb * n_pages) + s, F32) for s in shapes]
            + [pltpu.SemaphoreType.DMA((len(shapes), 2))])


def _dec_pv_t(e_pages, vt_pages, v_new, page):
    n_pages = len(vt_pages)
    parts = [_nt_dot(e_pages[p], vt_pages[p].astype(BF16)) for p in range(n_pages)]
    parts.append(jnp.dot(e_pages[n_pages], _tail_page(v_new, page).astype(BF16), preferred_element_type=F32))
    return _sum_list(parts)


def _dec_head_outputs(acc, l, n_new):
    grp = lax.broadcasted_iota(jnp.int32, acc.shape, 0) % DEC_GROUPS
    vhead = lax.broadcasted_iota(jnp.int32, acc.shape, 1) // HEAD_DIM
    return _collapse_groups(jnp.where(grp == vhead, acc / l, 0.0), n_new)


def _dec_b_kernel(pt_ref, q_ref, qf_ref, kn_ref, vn_ref, kt_hbm, vt_hbm, o_ref, kbuf, vbuf, sem,
                  *, n_pages, n_new, bpb, layer):
    fetch = _PageFetch(pt_ref, (kt_hbm, vt_hbm), (kbuf, vbuf), sem, layer, n_pages, bpb)
    fetch.begin()
    page = kbuf.shape[3]
    pages_per_block = MOBA_BLOCK // page
    n_blocks = n_pages // pages_per_block
    lane_id = lax.broadcasted_iota(jnp.int32, (WIDTH_B, n_blocks), 1)
    for u in range(bpb):
        q = q_ref[u]
        rows = q.shape[0]
        kt_f32 = []
        for p in range(n_pages):
            kt_f32.append(fetch.page(0, u, p))
            fetch.start_next_early(u, p)
        kmean_t = jnp.zeros((WIDTH_B, n_blocks), F32)
        for j in range(n_blocks):
            tot = _sum_list(kt_f32[j * pages_per_block:(j + 1) * pages_per_block])
            col = jnp.sum(tot, axis=1, keepdims=True) * (1.0 / MOBA_BLOCK)
            kmean_t = jnp.where(lane_id == j, col, kmean_t)
        gate = jnp.dot(qf_ref[u], kmean_t, precision=HIGHEST, preferred_element_type=F32)
        sel = _topk_block_mask(gate, jnp.full(gate.shape, True), 1)
        s_pages = []
        for p in range(n_pages):
            s = jnp.dot(q, kt_f32[p].astype(BF16), preferred_element_type=F32)
            j = p // pages_per_block
            s_pages.append(jnp.where(sel[:, j:j + 1] > 0.5, s, -jnp.inf))
        s_tail = _nt_dot(q, _tail_page(kn_ref[u], page).astype(BF16))
        s_pages.append(jnp.where(_tail_mask(rows, page, n_new), s_tail, -jnp.inf))
        e_pages, l = _dec_softmax(s_pages)
        acc = _dec_pv_t(e_pages, [fetch.page(1, u, p) for p in range(n_pages)], vn_ref[u], page)
        o_ref[u] = _dec_head_outputs(acc, l, n_new).astype(o_ref.dtype)
    fetch.end()


def _dec_b(page_table, qbd, qbd_f32, kv_new, cache_kt, cache_vt, layer, n_new, bpb):
    nb, rows, _ = qbd.shape
    n_pages = page_table.shape[1]
    page = cache_kt.shape[3]
    pad = kv_new.shape[1]
    off = 2 * WIDTH_A // WIDTH_B
    hbm = pl.BlockSpec(memory_space=pl.ANY)
    grid_spec = pltpu.PrefetchScalarGridSpec(
        num_scalar_prefetch=1,
        grid=(nb // bpb,),
        in_specs=[_batch_spec(bpb, rows, WIDTH_B), _batch_spec(bpb, rows, WIDTH_B),
                  _batch_spec(bpb, pad, WIDTH_B, off), _batch_spec(bpb, pad, WIDTH_B, off + 1), hbm, hbm],
        out_specs=_batch_spec(bpb, n_new, WIDTH_B),
        scratch_shapes=_page_fetch_scratch(bpb, n_pages, [(WIDTH_B, page), (WIDTH_B, page)]),
    )
    return pl.pallas_call(
        functools.partial(_dec_b_kernel, n_pages=n_pages, n_new=n_new, bpb=bpb, layer=layer),
        grid_spec=grid_spec,
        out_shape=jax.ShapeDtypeStruct((nb, n_new, WIDTH_B), BF16),
        compiler_params=_params("arbitrary"),
    )(page_table, qbd, qbd_f32, kv_new, kv_new, cache_kt, cache_vt)


def _split3(x):
    hi = x.astype(BF16)
    r1 = x - hi.astype(F32)
    mid = r1.astype(BF16)
    lo = (r1 - mid.astype(F32)).astype(BF16)
    return hi, mid, lo


def _dec_c_kernel(pt_ref, q_ref, kn_ref, vn_ref, lfn_ref, tri_ref, prefix_ref, lf_ref, kt_hbm, vt_hbm, o_ref,
                  kbuf, vbuf, sem, *, n_pages, n_new, bpb, layer):
    fetch = _PageFetch(pt_ref, (kt_hbm, vt_hbm), (kbuf, vbuf), sem, layer, n_pages, bpb)
    fetch.begin()
    first = pl.program_id(0) * bpb
    page = kbuf.shape[3]
    tri = tri_ref[...]
    for u in range(bpb):
        q = q_ref[u]
        rows = q.shape[0]
        slots = prefix_ref.shape[0] // SUBLANES
        lf_all = jnp.concatenate([lf_ref[pt_ref[first + u, p]] for p in range(n_pages)] + [lfn_ref[u]]
                                 + [jnp.zeros(((slots - n_pages - 1) * SUBLANES, page), F32)], axis=0)
        both = _sum_list([jnp.dot(part, tri, preferred_element_type=F32) for part in _split3(lf_all)])
        within, totals = both[:, :page], both[:, page:]
        earlier = _sum_list([jnp.dot(prefix_ref[...], part, preferred_element_type=F32) for part in _split3(totals)])
        cum_all = within + earlier
        cum_pages = [cum_all[p * SUBLANES:(p + 1) * SUBLANES] for p in range(n_pages + 1)]
        cum_new = jnp.concatenate([cum_pages[-1]] * n_new, axis=0)
        tok = lax.broadcasted_iota(jnp.int32, (rows, page), 0) // DEC_GROUPS
        key = lax.broadcasted_iota(jnp.int32, (rows, page), 1)
        f_query = jnp.sum(jnp.where(key == tok, cum_new, 0.0), axis=-1, keepdims=True)
        s_pages = []
        for p in range(n_pages + 1):
            if p < n_pages:
                s = jnp.dot(q, fetch.page(0, u, p).astype(BF16), preferred_element_type=F32)
                fetch.start_next_early(u, p)
            else:
                s = _nt_dot(q, _tail_page(kn_ref[u], page).astype(BF16))
            f_key = jnp.concatenate([cum_pages[p]] * n_new, axis=0)
            s_pages.append(s + (f_query - f_key) * LOG2E)
        s_pages[-1] = jnp.where(_tail_mask(rows, page, n_new), s_pages[-1], -jnp.inf)
        e_pages, l = _dec_softmax(s_pages)
        acc = _dec_pv_t(e_pages, [fetch.page(1, u, p) for p in range(n_pages)], vn_ref[u], page)
        o_ref[u] = _dec_head_outputs(acc, l, n_new).astype(o_ref.dtype)
    fetch.end()


def _dec_c(page_table, qbd, kv_new, lf_new_t, cache_kt, cache_vt, cache_lf_t, layer, n_new, bpb):
    nb, rows, _ = qbd.shape
    n_pages = page_table.shape[1]
    page = cache_kt.shape[3]
    pad = kv_new.shape[1]
    off = (2 * WIDTH_A + 2 * WIDTH_B) // WIDTH_C
    upper = lax.broadcasted_iota(jnp.int32, (page, page), 0) <= lax.broadcasted_iota(jnp.int32, (page, page), 1)
    tri = jnp.concatenate([upper, jnp.full((page, page), True)], axis=1).astype(BF16)
    n_rows = 2 * LANES
    assert (n_pages + 1) * SUBLANES <= n_rows
    r = lax.broadcasted_iota(jnp.int32, (n_rows, n_rows), 0)
    r2 = lax.broadcasted_iota(jnp.int32, (n_rows, n_rows), 1)
    prefix = ((r % SUBLANES == r2 % SUBLANES) & (r2 // SUBLANES < r // SUBLANES)).astype(BF16)
    hbm = pl.BlockSpec(memory_space=pl.ANY)
    grid_spec = pltpu.PrefetchScalarGridSpec(
        num_scalar_prefetch=1,
        grid=(nb // bpb,),
        in_specs=[_batch_spec(bpb, rows, WIDTH_C), _batch_spec(bpb, pad, WIDTH_C, off), _batch_spec(bpb, pad, WIDTH_C, off + 1),
                  _batch_spec(bpb, SUBLANES, page),
                  pl.BlockSpec((page, 2 * page), lambda bi, pt: (0, 0)),
                  pl.BlockSpec((n_rows, n_rows), lambda bi, pt: (0, 0)),
                  pl.BlockSpec((cache_lf_t.shape[0], None, SUBLANES, page), lambda bi, pt: (0, layer, 0, 0)), hbm, hbm],
        out_specs=_batch_spec(bpb, n_new, WIDTH_C),
        scratch_shapes=_page_fetch_scratch(bpb, n_pages, [(WIDTH_C, page), (WIDTH_C, page)]),
    )
    return pl.pallas_call(
        functools.partial(_dec_c_kernel, n_pages=n_pages, n_new=n_new, bpb=bpb, layer=layer),
        grid_spec=grid_spec,
        out_shape=jax.ShapeDtypeStruct((nb, n_new, WIDTH_C), BF16),
        compiler_params=_params("arbitrary"),
    )(page_table, qbd, kv_new, kv_new, lf_new_t, tri, prefix, cache_lf_t, cache_kt, cache_vt)


def _merge_kernel(x_ref, sc_ref, sh_ref, gate_ref, g0_ref, g1_ref, ya_ref, yb_ref, yc_ref,
                  wg_ref, wa_ref, wb_ref, wc_ref, wo_ref, o_ref):
    x = x_ref[...]
    d = x.shape[1]
    h = _rms(x, g0_ref[...]) * (1.0 + sc_ref[...]) + sh_ref[...]
    gl = jnp.dot(h.astype(BF16), wg_ref[...], preferred_element_type=F32)
    merged = (_sigmoid(gl[:, :d]) * jnp.dot(ya_ref[...], wa_ref[...], preferred_element_type=F32)
              + _sigmoid(gl[:, d:2 * d]) * jnp.dot(yb_ref[...], wb_ref[...], preferred_element_type=F32)
              + _sigmoid(gl[:, 2 * d:]) * jnp.dot(yc_ref[...], wc_ref[...], preferred_element_type=F32))
    o = jnp.dot(merged.astype(BF16), wo_ref[...], preferred_element_type=F32)
    o_ref[...] = x + gate_ref[...] * _rms(o, g1_ref[...])


def _merge(x2, mod, norm_g4, ya, yb, yc, wg, wa, wb, wc, wo, layer, *, rows_per_seq):
    n, d = x2.shape
    tm = min(ROW_TILE, n)
    row = lambda w: pl.BlockSpec((tm, w), lambda i: (i, 0))
    parts = [_mod_spec(mod, layer, k, tm, rows_per_seq) for k in (1, 0, 2)]
    return pl.pallas_call(
        _merge_kernel,
        grid=(n // tm,),
        in_specs=[row(d)] + parts + [_norm_spec(norm_g4, layer, 0), _norm_spec(norm_g4, layer, 1),
                                     row(WIDTH_A), row(WIDTH_B), row(WIDTH_C)]
        + [_layer_spec(w, layer) for w in (wg, wa, wb, wc, wo)],
        out_specs=row(d),
        out_shape=jax.ShapeDtypeStruct((n, d), F32),
        compiler_params=_params("arbitrary"),
    )(x2, mod, mod, mod, norm_g4, norm_g4, ya, yb, yc, wg, wa, wb, wc, wo)


def _ffn_kernel(x_ref, sc_ref, sh_ref, gate_ref, g2_ref, g3_ref, w1_ref, w2_ref, o_ref, *, chunk):
    x = x_ref[...]
    h = (_rms(x, g2_ref[...]) * (1.0 + sc_ref[...]) + sh_ref[...]).astype(BF16)
    d_ff = w1_ref.shape[1]
    acc = None
    for j in range(d_ff // chunk):
        a = jnp.dot(h, w1_ref[:, j * chunk:(j + 1) * chunk], preferred_element_type=F32)
        a = jnp.square(jnp.maximum(a, 0.0)).astype(BF16)
        part = jnp.dot(a, w2_ref[j * chunk:(j + 1) * chunk, :], preferred_element_type=F32)
        acc = part if acc is None else acc + part
    o_ref[...] = x + gate_ref[...] * _rms(acc, g3_ref[...])


def _ffn(x2, mod, norm_g4, w1, w2, layer, *, rows_per_seq):
    n, d = x2.shape
    tm = min(ROW_TILE, n)
    row = pl.BlockSpec((tm, d), lambda i: (i, 0))
    parts = [_mod_spec(mod, layer, k, tm, rows_per_seq) for k in (4, 3, 5)]
    return pl.pallas_call(
        functools.partial(_ffn_kernel, chunk=min(1024, w1.shape[2])),
        grid=(n // tm,),
        in_specs=[row] + parts + [_norm_spec(norm_g4, layer, 2), _norm_spec(norm_g4, layer, 3),
                                  _layer_spec(w1, layer), _layer_spec(w2, layer)],
        out_specs=row,
        out_shape=jax.ShapeDtypeStruct((n, d), F32),
        compiler_params=_params("arbitrary"),
    )(x2, mod, mod, mod, norm_g4, norm_g4, w1, w2)


def _rope_tables(pos):
    half = HEAD_DIM // 2
    inv = jnp.power(ROPE_THETA, -jnp.arange(half, dtype=F32) * 2.0 / HEAD_DIM)
    ang = pos.astype(F32)[:, None] * inv[None, :]
    cos, sin = jnp.cos(ang), jnp.sin(ang)
    reps = LANES // HEAD_DIM
    return jnp.tile(jnp.concatenate([cos, cos], axis=1), (1, reps)), jnp.tile(jnp.concatenate([-sin, sin], axis=1), (1, reps))


def _block_diag_queries(q, width):
    b, s, _ = q.shape
    grp = jnp.arange(DEC_GROUPS)[:, None]
    col = jnp.arange(width)[None, :] // HEAD_DIM
    keep = (grp == col)[None, None]
    out = jnp.where(keep, q[:, :, None, :], jnp.zeros((), q.dtype))
    return out.reshape(b, s * DEC_GROUPS, width)


def _pad_rows(a, rows):
    return jnp.pad(a, ((0, 0), (0, rows - a.shape[1]), (0, 0)))


def kernel(x_prompt, x_sample, cache_a_k, cache_a_v, cache_b_k, cache_b_v, cache_c_k, cache_c_v, cache_c_logf,
           page_table, c_prompt, c_sample, w_mod, b_mod, norm_g, w_in, b_forget, diff_lambda, diff_subln_g,
           w_branch_a, w_branch_b, w_branch_c, w_out, w_ff1, w_ff2):
    bp, t, d = x_prompt.shape
    bs, s_new, _ = x_sample.shape
    depth = w_mod.shape[0]
    n_pool, _, page = cache_a_k.shape[:3]
    n_pages = page_table.shape[1]
    past = n_pages * page
    assert t % MOBA_BLOCK == 0 and past % MOBA_BLOCK == 0 and MOBA_BLOCK % page == 0
    assert s_new <= SUBLANES and (bs * s_new) % SUBLANES == 0 and ATTN_BLOCK == ROW_TILE == MOBA_BLOCK
    dec_bpb = 2 if bs % 2 == 0 else 1

    w_qkv = w_in[:, :, :QKV_WIDTH].astype(BF16)
    w_f = jnp.pad(w_in[:, :, QKV_WIDTH:QKV_WIDTH + N_HEADS], ((0, 0), (0, 0), (0, LANES - N_HEADS))).astype(BF16)
    w_g = w_in[:, :, QKV_WIDTH + N_HEADS:].astype(BF16)
    b_f = jnp.pad(b_forget, ((0, 0), (0, LANES - N_HEADS))).reshape(depth, 1, LANES)
    wa, wb, wc = w_branch_a.astype(BF16), w_branch_b.astype(BF16), w_branch_c.astype(BF16)
    wo, w1, w2 = w_out.astype(BF16), w_ff1.astype(BF16), w_ff2.astype(BF16)
    norm_g4 = norm_g.reshape(depth, 4, 1, d)
    subln = diff_subln_g.reshape(depth, 1, 2 * HEAD_DIM)

    ca_kt = jnp.transpose(cache_a_k, (0, 1, 3, 4, 5, 2)).reshape(n_pool, depth, WIDTH_A, page)
    ca_v = cache_a_v.reshape(n_pool, depth, page * N_HEADS, 2 * HEAD_DIM)
    cb_kt = jnp.transpose(cache_b_k, (0, 1, 3, 4, 2)).reshape(n_pool, depth, WIDTH_B, page)
    cb_vt = jnp.transpose(cache_b_v, (0, 1, 3, 4, 2)).reshape(n_pool, depth, WIDTH_B, page)
    cc_kt = jnp.transpose(cache_c_k, (0, 1, 3, 4, 2)).reshape(n_pool, depth, WIDTH_C, page)
    cc_vt = jnp.transpose(cache_c_v, (0, 1, 3, 4, 2)).reshape(n_pool, depth, WIDTH_C, page)
    cc_lf = jnp.pad(jnp.swapaxes(cache_c_logf, 2, 3), ((0, 0), (0, 0), (0, SUBLANES - N_HEADS), (0, 0)))

    cos_p, sin_p = _rope_tables(jnp.arange(t))
    cos_s, sin_s = _rope_tables(jnp.tile(past + jnp.arange(s_new), bs))

    n_s = bs * s_new
    mod = _modulation(jnp.concatenate([c_sample, c_prompt], axis=0), w_mod, b_mod)
    mod_s = jnp.repeat(mod[:, :bs], s_new, axis=1)
    mod_p = mod[:, bs:].reshape(depth, bp, 6, 1, d)

    feat_major = lambda w: (bp, depth, w, t)
    cache_shapes = [feat_major(WIDTH_A), (bp, depth, t * N_HEADS, 2 * HEAD_DIM),
                    feat_major(WIDTH_B), feat_major(WIDTH_B), feat_major(WIDTH_C), feat_major(WIDTH_C), feat_major(N_HEADS)]
    caches = None

    xp = x_prompt.reshape(bp * t, d)
    xs = x_sample.reshape(n_s, d)
    rows_s = []
    for l in range(depth):
        lam_init = 0.8 - 0.6 * math.exp(-0.3 * l)
        dl, sg = diff_lambda[l], subln[l]

        q, qbf, kb16, vt16, cum_rep, cum_t, kmean, caches = _inproj_prompt(
            xp, mod_p, norm_g4, cos_p, sin_p, w_qkv, w_f, b_f, caches, cache_shapes, l, seq=t)
        q3 = q.reshape(bp, t, -1)
        k3 = kb16.reshape(bp, t, -1)
        ya = _attn_a(q3, k3, vt16, dl, sg, lam_init)
        yb = _attn_b(q3, qbf.reshape(bp, t, WIDTH_B), k3, vt16, kmean.reshape(bp, t // MOBA_BLOCK, WIDTH_B))
        yc = _attn_c(q3, k3, vt16, cum_rep.reshape(bp, t, -1), cum_t)
        xp = _merge(xp, mod_p, norm_g4, ya.reshape(bp * t, -1), yb.reshape(bp * t, -1), yc.reshape(bp * t, -1),
                    w_g, wa, wb, wc, wo, l, rows_per_seq=t)
        xp = _ffn(xp, mod_p, norm_g4, w1, w2, l, rows_per_seq=t)

        q, qbf, kvf, lf = _inproj_sample(xs, mod_s, norm_g4, cos_s, sin_s, w_qkv, w_f, b_f, l)
        q3 = q.reshape(bs, s_new, -1)
        kvf3 = kvf.reshape(bs, s_new, -1)
        kv_new = _pad_rows(kvf3, SUBLANES)
        qa_bd = _block_diag_queries(q3[:, :, :WIDTH_A], WIDTH_A)
        qb_bd = _block_diag_queries(q3[:, :, WIDTH_A:WIDTH_A + WIDTH_B], WIDTH_B)
        qc_bd = _block_diag_queries(q3[:, :, WIDTH_A + WIDTH_B:], WIDTH_C)
        qbf_bd = _block_diag_queries(qbf.reshape(bs, s_new, WIDTH_B), WIDTH_B)
        lf3 = lf.reshape(bs, s_new, LANES)
        lf_new_t = jnp.pad(jnp.swapaxes(lf3[:, :, :N_HEADS], 1, 2),
                           ((0, 0), (0, SUBLANES - N_HEADS), (0, page - s_new)))
        ya = _dec_a(page_table, qa_bd, kv_new, ca_kt, ca_v, dl, sg, l, lam_init, s_new, 1)
        yb = _dec_b(page_table, qb_bd, qbf_bd, kv_new, cb_kt, cb_vt, l, s_new, dec_bpb)
        yc = _dec_c(page_table, qc_bd, kv_new, lf_new_t, cc_kt, cc_vt, cc_lf, l, s_new, dec_bpb)
        xs = _merge(xs, mod_s, norm_g4, ya.reshape(n_s, -1), yb.reshape(n_s, -1), yc.reshape(n_s, -1),
                    w_g, wa, wb, wc, wo, l, rows_per_seq=s_new)
        xs = _ffn(xs, mod_s, norm_g4, w1, w2, l, rows_per_seq=s_new)
        rows_s.append((kvf3, lf3))

    akt, av, bkt, bvt, ckt, cvt, lft = caches
    tok_major = lambda c: jnp.transpose(c.reshape(bp, depth, N_HEADS, HEAD_DIM, t), (0, 1, 4, 2, 3))
    prompt_rows = (jnp.transpose(akt.reshape(bp, depth, 2, N_HEADS, HEAD_DIM, t), (0, 1, 5, 2, 3, 4)),
                   av.reshape(bp, depth, t, N_HEADS, 2 * HEAD_DIM),
                   tok_major(bkt), tok_major(bvt), tok_major(ckt), tok_major(cvt), jnp.swapaxes(lft, 2, 3))

    kv = jnp.stack([r[0] for r in rows_s], axis=1)
    o = 0
    a_k = kv[..., o:o + WIDTH_A].reshape(bs, depth, s_new, 2, N_HEADS, HEAD_DIM); o += WIDTH_A
    a_v = kv[..., o:o + WIDTH_A].reshape(bs, depth, s_new, N_HEADS, 2 * HEAD_DIM); o += WIDTH_A
    b_k = kv[..., o:o + WIDTH_B].reshape(bs, depth, s_new, N_HEADS, HEAD_DIM); o += WIDTH_B
    b_v = kv[..., o:o + WIDTH_B].reshape(bs, depth, s_new, N_HEADS, HEAD_DIM); o += WIDTH_B
    c_k = kv[..., o:o + WIDTH_C].reshape(bs, depth, s_new, N_HEADS, HEAD_DIM); o += WIDTH_C
    c_v = kv[..., o:o + WIDTH_C].reshape(bs, depth, s_new, N_HEADS, HEAD_DIM)
    c_lf = jnp.stack([r[1][:, :, :N_HEADS] for r in rows_s], axis=1)
    return ((xp.reshape(bp, t, d), xs.reshape(bs, s_new, d)) + prompt_rows + (a_k, a_v, b_k, b_v, c_k, c_v, c_lf))
```
